```python
import math
import jax, jax.numpy as jnp
from jax import lax
import numpy as np

D_MODEL = 1024
BATCH = 16
SEQ = 2048
DEPTH = 2

DA_HEADS = 8
DA_HEAD_DIM = 64
DA_WIDTH = DA_HEADS * 2 * DA_HEAD_DIM
SSD_WIDTH = D_MODEL
SSD_HEAD_DIM = 64
SSD_HEADS = SSD_WIDTH // SSD_HEAD_DIM
SSD_GROUPS = 2
SSD_STATE = 128
SSD_CONV = 4
SSD_CHUNK = 128
SSD_CONV_DIM = SSD_WIDTH + 2 * SSD_GROUPS * SSD_STATE
SW_HEADS = 16
SW_KV_HEADS = 2
SW_GQ = SW_HEADS // SW_KV_HEADS
SW_HEAD_DIM = 64
SW_WIDTH = SW_HEADS * SW_HEAD_DIM
SW_KV_WIDTH = SW_KV_HEADS * SW_HEAD_DIM
WINDOW = 128
Q_BLOCK = 128
EPS = 1e-5

N_EVEN = (DEPTH + 1) // 2
N_ODD = DEPTH // 2
EVEN_IN = 4 * DA_WIDTH + SSD_WIDTH + SSD_CONV_DIM + SSD_HEADS
EVEN_SPLITS = [DA_WIDTH, 2 * DA_WIDTH, 3 * DA_WIDTH, 4 * DA_WIDTH,
               4 * DA_WIDTH + SSD_WIDTH, 4 * DA_WIDTH + SSD_WIDTH + SSD_CONV_DIM]
ODD_IN = 2 * SW_WIDTH + 2 * SW_KV_WIDTH
ODD_SPLITS = [SW_WIDTH, SW_WIDTH + SW_KV_WIDTH, SW_WIDTH + 2 * SW_KV_WIDTH]

kernel_name = "hybrid_diffattn_ssd_swa_block"


def rmsnorm(x, w):
    xf = x.astype(jnp.float32)
    y = xf * lax.rsqrt(jnp.mean(xf * xf, axis=-1, keepdims=True) + EPS)
    return (y * w.astype(jnp.float32)).astype(x.dtype)


def alibi_slopes(n):
    return jnp.exp2(-8.0 * jnp.arange(1, n + 1, dtype=jnp.float32) / n)


def diff_attention(q, k, v, lam, slopes):
    b, s = q.shape[:2]
    nb = s // Q_BLOCK
    scale = DA_HEAD_DIM ** -0.5
    qb = q.reshape(b, nb, Q_BLOCK, DA_HEADS, 2, DA_HEAD_DIM).transpose(1, 0, 2, 3, 4, 5)
    kpos = jnp.arange(s)

    def block(args):
        qi, n = args
        qpos = n * Q_BLOCK + jnp.arange(Q_BLOCK)
        dist = qpos[:, None] - kpos[None, :]
        logits = jnp.einsum('bqhmd,bshmd->bhmqs', qi, k).astype(jnp.float32) * scale
        logits = logits - slopes[None, :, None, None, None] * dist.astype(jnp.float32)[None, None, None]
        logits = jnp.where((dist >= 0)[None, None, None], logits, -jnp.inf)
        p = jax.nn.softmax(logits, axis=-1)
        attn = p[:, :, 0] - lam * p[:, :, 1]
        return jnp.einsum('bhqs,bshe->bqhe', attn.astype(v.dtype), v)

    out = lax.map(block, (qb, jnp.arange(nb)))
    return out.transpose(1, 0, 2, 3, 4).reshape(b, s, DA_HEADS, 2 * DA_HEAD_DIM)


def causal_dwconv(x, w, bias):
    c = x.shape[-1]
    y = lax.conv_general_dilated(x, w[:, None, :].astype(x.dtype), window_strides=(1,),
                                 padding=[(SSD_CONV - 1, 0)],
                                 dimension_numbers=('NWC', 'WIO', 'NWC'),
                                 feature_group_count=c)
    return y + bias.astype(x.dtype)


def segsum(a):
    cs = jnp.cumsum(a, axis=-1)
    diff = cs[..., :, None] - cs[..., None, :]
    n = a.shape[-1]
    mask = jnp.tril(jnp.ones((n, n), dtype=bool))
    return jnp.where(mask, diff, -jnp.inf)


def ssd_scan(xh, dt, A, bm, cm):
    b, s = xh.shape[:2]
    nc = s // SSD_CHUNK
    hpg = SSD_HEADS // SSD_GROUPS
    dtype = xh.dtype
    xdt = (xh * dt[..., None].astype(dtype)).reshape(b, nc, SSD_CHUNK, SSD_GROUPS, hpg, SSD_HEAD_DIM)
    bc = bm.reshape(b, nc, SSD_CHUNK, SSD_GROUPS, SSD_STATE)
    cc = cm.reshape(b, nc, SSD_CHUNK, SSD_GROUPS, SSD_STATE)
    dA = (dt * A).reshape(b, nc, SSD_CHUNK, SSD_GROUPS, hpg).transpose(0, 1, 3, 4, 2)
    cs = jnp.cumsum(dA, axis=-1)
    lmat = jnp.exp(segsum(dA)).astype(dtype)
    cb = jnp.einsum('bclgn,bcsgn->bcgls', cc, bc)
    y_diag = jnp.einsum('bcgjls,bcsgjp->bclgjp', cb[:, :, :, None] * lmat, xdt)
    decay_states = jnp.exp(cs[..., -1:] - cs).astype(dtype)
    states = jnp.einsum('bclgn,bcgjl,bclgjp->bcgjpn', bc, decay_states, xdt)
    chunk_decay = jnp.exp(cs[..., -1]).astype(dtype)

    def step(h, inp):
        st, dec = inp
        return h * dec[..., None, None] + st, h

    h0 = jnp.zeros((b, SSD_GROUPS, hpg, SSD_HEAD_DIM, SSD_STATE), dtype)
    _, prev = lax.scan(step, h0, (states.transpose(1, 0, 2, 3, 4, 5), chunk_decay.transpose(1, 0, 2, 3)))
    prev = prev.transpose(1, 0, 2, 3, 4, 5)
    y_off = jnp.einsum('bclgn,bcgjpn,bcgjl->bclgjp', cc, prev, jnp.exp(cs).astype(dtype))
    return (y_diag + y_off).reshape(b, s, SSD_HEADS, SSD_HEAD_DIM)


def sliding_window_attention(q, k, v, sinks, slopes):
    b, s = q.shape[:2]
    nb = s // WINDOW
    scale = SW_HEAD_DIM ** -0.5
    qb = q.reshape(b, nb, WINDOW, SW_KV_HEADS, SW_GQ, SW_HEAD_DIM)
    kb = k.reshape(b, nb, WINDOW, SW_KV_HEADS, SW_HEAD_DIM)
    vb = v.reshape(b, nb, WINDOW, SW_KV_HEADS, SW_HEAD_DIM)
    pad = ((0, 0), (1, 0), (0, 0), (0, 0), (0, 0))
    kk = jnp.concatenate([jnp.pad(kb, pad)[:, :-1], kb], axis=2)
    vv = jnp.concatenate([jnp.pad(vb, pad)[:, :-1], vb], axis=2)
    logits = jnp.einsum('bnqkgd,bnskd->bnkgqs', qb, kk).astype(jnp.float32) * scale
    qrel = jnp.arange(WINDOW) + WINDOW
    krel = jnp.arange(2 * WINDOW)
    dist = qrel[:, None] - krel[None, :]
    key_abs = (jnp.arange(nb) * WINDOW - WINDOW)[:, None] + krel[None, :]
    valid = ((dist >= 0) & (dist < WINDOW))[None] & (key_abs >= 0)[:, None, :]
    sl = slopes.reshape(SW_KV_HEADS, SW_GQ)[None, None, :, :, None, None]
    logits = logits - sl * dist.astype(jnp.float32)
    logits = jnp.where(valid[None, :, None, None], logits, -jnp.inf)
    sink = jnp.broadcast_to(sinks.astype(jnp.float32).reshape(SW_KV_HEADS, SW_GQ)[None, None, :, :, None, None],
                            logits.shape[:-1] + (1,))
    p = jax.nn.softmax(jnp.concatenate([logits, sink], axis=-1), axis=-1)[..., :-1]
    out = jnp.einsum('bnkgqs,bnskd->bnqkgd', p.astype(v.dtype), vv)
    return out.reshape(b, s, SW_WIDTH)


def even_layer(h, w_in, conv_w, conv_b, dt_bias, a_log, d_skip, ssd_norm_w,
               lq1, lk1, lq2, lk2, subln_w, w_out, lambda_init):
    b, s, _ = h.shape
    proj = h @ w_in
    q, k, v, g_a, z, xbc, dt_raw = jnp.split(proj, EVEN_SPLITS, axis=-1)
    q = q.reshape(b, s, DA_HEADS, 2, DA_HEAD_DIM)
    k = k.reshape(b, s, DA_HEADS, 2, DA_HEAD_DIM)
    v = v.reshape(b, s, DA_HEADS, 2 * DA_HEAD_DIM)
    f32 = jnp.float32
    lam = (jnp.exp(jnp.sum(lq1.astype(f32) * lk1.astype(f32)))
           - jnp.exp(jnp.sum(lq2.astype(f32) * lk2.astype(f32))) + lambda_init)
    attn = diff_attention(q, k, v, lam, alibi_slopes(DA_HEADS))
    attn = rmsnorm(attn, subln_w) * (1.0 - lambda_init)
    y_a = attn.reshape(b, s, DA_WIDTH) * jax.nn.silu(g_a)
    xbc = jax.nn.silu(causal_dwconv(xbc, conv_w, conv_b))
    xs, bm, cm = jnp.split(xbc, [SSD_WIDTH, SSD_WIDTH + SSD_GROUPS * SSD_STATE], axis=-1)
    xh = xs.reshape(b, s, SSD_HEADS, SSD_HEAD_DIM)
    dt = jax.nn.softplus(dt_raw.astype(f32) + dt_bias.astype(f32))
    A = -jnp.exp(a_log.astype(f32))
    y = ssd_scan(xh, dt, A, bm.reshape(b, s, SSD_GROUPS, SSD_STATE), cm.reshape(b, s, SSD_GROUPS, SSD_STATE))
    y = y + xh * d_skip[:, None].astype(xh.dtype)
    y = y.reshape(b, s, SSD_WIDTH) * jax.nn.silu(z)
    y_b = rmsnorm(y.reshape(b, s, SSD_GROUPS, SSD_WIDTH // SSD_GROUPS),
                  ssd_norm_w.reshape(SSD_GROUPS, SSD_WIDTH // SSD_GROUPS)).reshape(b, s, SSD_WIDTH)
    return jnp.concatenate([y_a, y_b], axis=-1) @ w_out


def odd_layer(h, w_in, b_in, sinks, w_out):
    b, s, _ = h.shape
    proj = h @ w_in + b_in
    q, k, v, g = jnp.split(proj, ODD_SPLITS, axis=-1)
    q = q.reshape(b, s, SW_KV_HEADS, SW_GQ, SW_HEAD_DIM)
    k = k.reshape(b, s, SW_KV_HEADS, SW_HEAD_DIM)
    v = v.reshape(b, s, SW_KV_HEADS, SW_HEAD_DIM)
    o = sliding_window_attention(q, k, v, sinks, alibi_slopes(SW_HEADS))
    return (o * jax.nn.silu(g)) @ w_out


def setup_inputs(seed: int = 0) -> dict:
    key = jax.random.key(seed)
    ks = jax.random.split(key, 24)
    f32 = jnp.float32

    def nrm(k, shape, scale):
        return jax.random.normal(k, shape, f32) * scale

    dt0 = jnp.exp(jax.random.uniform(ks[5], (N_EVEN, SSD_HEADS), f32, math.log(1e-3), math.log(1e-1)))
    return {
        "x": nrm(ks[0], (BATCH, SEQ, D_MODEL), 1.0),
        "norm_a": 1.0 + nrm(ks[1], (N_EVEN, D_MODEL), 0.02),
        "w_in_a": nrm(ks[2], (N_EVEN, D_MODEL, EVEN_IN), D_MODEL ** -0.5),
        "conv_w_a": nrm(ks[3], (N_EVEN, SSD_CONV, SSD_CONV_DIM), SSD_CONV ** -0.5),
        "conv_b_a": nrm(ks[4], (N_EVEN, SSD_CONV_DIM), 0.02),
        "dt_bias_a": dt0 + jnp.log(-jnp.expm1(-dt0)),
        "a_log_a": jnp.log(jax.random.uniform(ks[6], (N_EVEN, SSD_HEADS), f32, 1.0, 16.0)),
        "d_skip_a": 1.0 + nrm(ks[7], (N_EVEN, SSD_HEADS), 0.1),
        "ssd_norm_a": 1.0 + nrm(ks[8], (N_EVEN, SSD_WIDTH), 0.02),
        "lambda_q1_a": nrm(ks[9], (N_EVEN, DA_HEAD_DIM), 0.1),
        "lambda_k1_a": nrm(ks[10], (N_EVEN, DA_HEAD_DIM), 0.1),
        "lambda_q2_a": nrm(ks[11], (N_EVEN, DA_HEAD_DIM), 0.1),
        "lambda_k2_a": nrm(ks[12], (N_EVEN, DA_HEAD_DIM), 0.1),
        "subln_a": 1.0 + nrm(ks[13], (N_EVEN, 2 * DA_HEAD_DIM), 0.02),
        "w_out_a": nrm(ks[14], (N_EVEN, DA_WIDTH + SSD_WIDTH, D_MODEL), (DA_WIDTH + SSD_WIDTH) ** -0.5),
        "norm_c": 1.0 + nrm(ks[15], (N_ODD, D_MODEL), 0.02),
        "w_in_c": nrm(ks[16], (N_ODD, D_MODEL, ODD_IN), D_MODEL ** -0.5),
        "b_in_c": nrm(ks[17], (N_ODD, ODD_IN), 0.02),
        "sinks_c": nrm(ks[18], (N_ODD, SW_HEADS), 1.0),
        "w_out_c": nrm(ks[19], (N_ODD, SW_WIDTH, D_MODEL), SW_WIDTH ** -0.5),
        "final_norm": 1.0 + nrm(ks[20], (D_MODEL,), 0.02),
    }


def reference(x, norm_a, w_in_a, conv_w_a, conv_b_a, dt_bias_a, a_log_a, d_skip_a, ssd_norm_a,
              lambda_q1_a, lambda_k1_a, lambda_q2_a, lambda_k2_a, subln_a, w_out_a,
              norm_c, w_in_c, b_in_c, sinks_c, w_out_c, final_norm):
    for i in range(DEPTH):
        j = i // 2
        if i % 2 == 0:
            lambda_init = 0.8 - 0.6 * math.exp(-0.3 * i)
            x = x + even_layer(rmsnorm(x, norm_a[j]), w_in_a[j], conv_w_a[j], conv_b_a[j],
                               dt_bias_a[j], a_log_a[j], d_skip_a[j], ssd_norm_a[j],
                               lambda_q1_a[j], lambda_k1_a[j], lambda_q2_a[j], lambda_k2_a[j],
                               subln_a[j], w_out_a[j], lambda_init)
        else:
            x = x + odd_layer(rmsnorm(x, norm_c[j]), w_in_c[j], b_in_c[j], sinks_c[j], w_out_c[j])
    return rmsnorm(x, final_norm)
```

```python
import functools
import math

import jax
import jax.numpy as jnp
from jax import lax
from jax.experimental import pallas as pl
from jax.experimental.pallas import tpu as pltpu

F32 = jnp.float32
BF16 = jnp.bfloat16

D_MODEL = 1024
EPS = 1e-5

DA_HEADS = 8
DA_HEAD_DIM = 64
DA_WIDTH = DA_HEADS * 2 * DA_HEAD_DIM

SSD_WIDTH = 1024
SSD_HEAD_DIM = 64
SSD_HEADS = SSD_WIDTH // SSD_HEAD_DIM
SSD_GROUPS = 2
SSD_STATE = 128
SSD_CONV = 4
SSD_CHUNK = 128
SSD_BC = 2 * SSD_GROUPS * SSD_STATE

SW_HEADS = 16
SW_KV_HEADS = 2
SW_GQ = SW_HEADS // SW_KV_HEADS
SW_HEAD_DIM = 64
SW_WIDTH = SW_HEADS * SW_HEAD_DIM
WINDOW = 128

EVEN_MAIN = 4 * DA_WIDTH + SSD_WIDTH + SSD_WIDTH + SSD_BC
LANES = 128
TAIL = 8
NEG = -1e30

VMEM_LIMIT = 56 * 1024 * 1024


def _alibi_slope(i, n):
    return float(2.0 ** (-8.0 * (i + 1) / n))


def _sigmoid(x):
    return 1.0 / (1.0 + jnp.exp(-x))


def _silu(x):
    return x * _sigmoid(x)


def _params(*sem):
    return pltpu.CompilerParams(dimension_semantics=sem, vmem_limit_bytes=VMEM_LIMIT)


def _const_spec(shape):
    nd = len(shape)
    return pl.BlockSpec(shape, lambda *_: (0,) * nd, pipeline_mode=pl.Buffered(1))


INPROJ_TM = 512
INPROJ_TN = 512


def _inproj_even_kernel(x_ref, nw_ref, w_ref, wdt_ref, o_ref, dt_ref, hn_ref):
    x = x_ref[...]
    ms = jnp.mean(x * x, axis=-1, keepdims=True)
    hn_ref[...] = (x * lax.rsqrt(ms + EPS) * nw_ref[...]).astype(BF16)
    n = o_ref.shape[1]
    for j in range(n // INPROJ_TN):
        sl = slice(j * INPROJ_TN, (j + 1) * INPROJ_TN)
        o_ref[:, sl] = jnp.dot(hn_ref[...], w_ref[:, sl], preferred_element_type=F32).astype(BF16)
    dt_ref[...] = jnp.dot(hn_ref[...], wdt_ref[...], preferred_element_type=F32)


def _inproj_even(x2, norm_w, w_main, w_dt):
    m = x2.shape[0]
    n = w_main.shape[1]
    tm = INPROJ_TM
    return pl.pallas_call(
        _inproj_even_kernel,
        grid=(m // tm,),
        in_specs=[
            pl.BlockSpec((tm, D_MODEL), lambda i: (i, 0)),
            _const_spec((1, D_MODEL)),
            _const_spec((D_MODEL, n)),
            _const_spec((D_MODEL, LANES)),
        ],
        out_specs=[
            pl.BlockSpec((tm, n), lambda i: (i, 0)),
            pl.BlockSpec((tm, LANES), lambda i: (i, 0)),
        ],
        out_shape=[
            jax.ShapeDtypeStruct((m, n), BF16),
            jax.ShapeDtypeStruct((m, LANES), F32),
        ],
        scratch_shapes=[pltpu.VMEM((tm, D_MODEL), BF16)],
        compiler_params=_params("arbitrary"),
        name="inproj_even",
    )(x2, norm_w, w_main, w_dt)


DA_TQ = 256


def _diff_attn_kernel(lam_ref, sub_ref, q_ref, k_ref, v_ref, g_ref, o_ref, rel_ref, *, lambda_init):
    tq = DA_TQ
    qi = pl.program_id(1)
    scale = DA_HEAD_DIM ** -0.5

    lp = lam_ref[...]
    lam = (jnp.exp(jnp.sum(lp[0:1] * lp[1:2], axis=-1, keepdims=True))
           - jnp.exp(jnp.sum(lp[2:3] * lp[3:4], axis=-1, keepdims=True)) + lambda_init)

    row = lax.broadcasted_iota(jnp.int32, (tq, tq), 0)
    col = lax.broadcasted_iota(jnp.int32, (tq, tq), 1)
    rel_ref[...] = (col - row).astype(F32)
    lane = lax.broadcasted_iota(jnp.int32, (tq, LANES), 1)
    first_map = lane < DA_HEAD_DIM

    for h in range(DA_HEADS):
        slope = _alibi_slope(h, DA_HEADS)
        hs = slice(h * LANES, (h + 1) * LANES)
        q = q_ref[0, :, hs] * jnp.asarray(scale, BF16)
        zero = jnp.zeros_like(q)
        qq = jnp.concatenate([jnp.where(first_map, q, zero), jnp.where(first_map, zero, q)], axis=0)

        def tile(j, carry, diagonal):
            m_old, l_old, acc = carry
            k = k_ref[0, pl.ds(pl.multiple_of(j * tq, tq), tq), hs]
            v = v_ref[0, pl.ds(pl.multiple_of(j * tq, tq), tq), hs]
            s = lax.dot_general(qq, k, (((1,), (1,)), ((), ())), preferred_element_type=F32)
            bias = rel_ref[...] * slope
            if diagonal:
                bias = jnp.where(rel_ref[...] <= 0.0, bias, NEG)
            s = s + jnp.concatenate([bias, bias], axis=0)
            shift = (slope * tq) * (j - qi).astype(F32)
            m_new = jnp.maximum(m_old, jnp.max(s, axis=-1, keepdims=True) + shift)
            alpha = jnp.exp(m_old - m_new)
            p = jnp.exp(s - (m_new - shift))
            l_new = alpha * l_old + jnp.sum(p, axis=-1, keepdims=True)
            acc = alpha * acc + jnp.dot(p.astype(BF16), v, preferred_element_type=F32)
            return m_new, l_new, acc

        init = (jnp.full((2 * tq, 1), NEG, F32), jnp.zeros((2 * tq, 1), F32),
                jnp.zeros((2 * tq, LANES), F32))
        carry = lax.fori_loop(0, qi, lambda j, c: tile(j, c, False), init)
        _, l_fin, acc = tile(qi, carry, True)

        o = acc / l_fin
        attn = o[:tq] - lam * o[tq:]
        y = attn * lax.rsqrt(jnp.mean(attn * attn, axis=-1, keepdims=True) + EPS) * sub_ref[...]
        y = y * (1.0 - lambda_init)
        g = g_ref[0, :, hs].astype(F32)
        o_ref[0, :, hs] = (y * _silu(g)).astype(BF16)


def _diff_attn(proj, lam_params, subln_w, lambda_init):
    b, s, _ = proj.shape
    tq = DA_TQ
    kern = functools.partial(_diff_attn_kernel, lambda_init=lambda_init)
    return pl.pallas_call(
        kern,
        grid=(b, s // tq),
        in_specs=[
            _const_spec((4, DA_HEAD_DIM)),
            _const_spec((1, 2 * DA_HEAD_DIM)),
            pl.BlockSpec((1, tq, DA_WIDTH), lambda bi, qi: (bi, qi, 0)),
            pl.BlockSpec((1, s, DA_WIDTH), lambda bi, qi: (bi, 0, 1)),
            pl.BlockSpec((1, s, DA_WIDTH), lambda bi, qi: (bi, 0, 2)),
            pl.BlockSpec((1, tq, DA_WIDTH), lambda bi, qi: (bi, qi, 3)),
        ],
        out_specs=pl.BlockSpec((1, tq, DA_WIDTH), lambda bi, qi: (bi, qi, 0)),
        out_shape=jax.ShapeDtypeStruct((b, s, DA_WIDTH), BF16),
        scratch_shapes=[pltpu.VMEM((tq, tq), F32)],
        compiler_params=_params("arbitrary", "arbitrary"),
        name="diff_attn",
    )(lam_params, subln_w, proj, proj, proj, proj)


def _split3(x):
    hi = x.astype(BF16)
    r1 = x - hi.astype(F32)
    mid = r1.astype(BF16)
    lo = (r1 - mid.astype(F32)).astype(BF16)
    return hi, mid, lo


def _expand_heads(x, r_ref):
    r = r_ref[...]
    hi, mid, lo = _split3(x)
    return (jnp.dot(hi, r, preferred_element_type=F32) + jnp.dot(mid, r, preferred_element_type=F32)
            + jnp.dot(lo, r, preferred_element_type=F32))


def _conv_silu(raw, ext_ref, tail_ref, w_ref, b_ref):
    rows = raw.shape[0]
    ext_ref[0:TAIL, :] = tail_ref[...]
    ext_ref[TAIL:TAIL + rows, :] = raw
    tail_ref[...] = raw[rows - TAIL:rows, :]
    acc = b_ref[...] + w_ref[0:1, :] * ext_ref[TAIL - 3:TAIL - 3 + rows, :]
    for k in range(1, SSD_CONV):
        acc = acc + w_ref[k:k + 1, :] * ext_ref[TAIL - 3 + k:TAIL - 3 + k + rows, :]
    return _silu(acc)


def _ssd_kernel(z_ref, xs_ref, bc_ref, dt_ref, cwx_ref, cbx_ref, cwb_ref, cbb_ref, dtb_ref, alog_ref,
                dskip_ref, nw_ref, r_ref, o_ref,
                extx_ref, extb_ref, tailx_ref, tailb_ref, state_ref):
    L = SSD_CHUNK
    hpg = SSD_HEADS // SSD_GROUPS
    gw = SSD_WIDTH // SSD_GROUPS

    @pl.when(pl.program_id(1) == 0)
    def _():
        tailx_ref[...] = jnp.zeros_like(tailx_ref)
        tailb_ref[...] = jnp.zeros_like(tailb_ref)
        state_ref[...] = jnp.zeros_like(state_ref)

    xs = _conv_silu(xs_ref[0].astype(F32), extx_ref, tailx_ref, cwx_ref, cbx_ref)
    bc = _conv_silu(bc_ref[0].astype(F32), extb_ref, tailb_ref, cwb_ref, cbb_ref)
    bcb = bc.astype(BF16)

    draw = dt_ref[0] + dtb_ref[...]
    dt = jnp.maximum(draw, 0.0) + jnp.log1p(jnp.exp(-jnp.abs(draw)))
    a = -jnp.exp(alog_ref[...])
    da = dt * a

    row = lax.broadcasted_iota(jnp.int32, (L, L), 0)
    col = lax.broadcasted_iota(jnp.int32, (L, L), 1)
    tril = row >= col
    hi, mid, lo = _split3(da)
    ones = tril.astype(BF16)
    cs = (jnp.dot(ones, hi, preferred_element_type=F32) + jnp.dot(ones, mid, preferred_element_type=F32)
          + jnp.dot(ones, lo, preferred_element_type=F32))
    cs_t = cs.T
    cs_last = cs[L - 1:L, :]

    dt_e = _expand_heads(dt, r_ref)
    cs_e = _expand_heads(cs, r_ref)
    cs_last_e = cs_e[L - 1:L, :]
    xdt = xs * dt_e
    xdt_b = xdt.astype(BF16)
    w_state = (xdt * jnp.exp(cs_last_e - cs_e)).astype(BF16)
    e_in = jnp.exp(cs_e)
    chunk_decay = jnp.exp(cs_last_e)

    lane = lax.broadcasted_iota(jnp.int32, (L, LANES), 1)
    first_head = lane < SSD_HEAD_DIM
    zeros_b = jnp.zeros((L, LANES), BF16)

    y_parts = []
    for g in range(SSD_GROUPS):
        bg = bcb[:, g * SSD_STATE:(g + 1) * SSD_STATE]
        cg = bcb[:, (SSD_GROUPS + g) * SSD_STATE:(SSD_GROUPS + g + 1) * SSD_STATE]
        cb = lax.dot_general(cg, bg, (((1,), (1,)), ((), ())), preferred_element_type=F32)
        for jp in range(hpg // 2):
            pair = None
            for half in range(2):
                hh = g * hpg + 2 * jp + half
                seg = cs[:, hh:hh + 1] - cs_t[hh:hh + 1, :]
                lm = jnp.exp(jnp.where(tril, seg, NEG))
                mj = (cb * lm).astype(BF16)
                xp = xdt_b[:, (g * hpg + 2 * jp) * SSD_HEAD_DIM:(g * hpg + 2 * jp + 2) * SSD_HEAD_DIM]
                xp = jnp.where(first_head, xp, zeros_b) if half == 0 else jnp.where(first_head, zeros_b, xp)
                t = jnp.dot(mj, xp, preferred_element_type=F32)
                pair = t if pair is None else pair + t
            y_parts.append(pair)
        gs = slice(g * gw, (g + 1) * gw)
        prev = state_ref[g]
        y_off = jnp.dot(cg, prev.astype(BF16), preferred_element_type=F32) * e_in[:, gs]
        st = lax.dot_general(bg, w_state[:, gs], (((0,), (0,)), ((), ())), preferred_element_type=F32)
        state_ref[g] = prev * chunk_decay[:, gs] + st
        y_parts.append(y_off)

    nq = hpg // 2
    y_diag = jnp.concatenate(y_parts[0:nq] + y_parts[nq + 1:2 * nq + 1], axis=-1)
    y_off = jnp.concatenate([y_parts[nq], y_parts[2 * nq + 1]], axis=-1)
    y = y_diag + y_off + xs * dskip_ref[...]
    y = y * _silu(z_ref[0].astype(F32))
    outs = []
    for g in range(SSD_GROUPS):
        yg = y[:, g * gw:(g + 1) * gw]
        outs.append(yg * lax.rsqrt(jnp.mean(yg * yg, axis=-1, keepdims=True) + EPS))
    o_ref[0] = (jnp.concatenate(outs, axis=-1) * nw_ref[...]).astype(BF16)


def _ssd(proj, dt_raw, cw_x, cb_x, cw_b, cb_b, dt_bias, a_log, d_skip_e, norm_w, expand):
    b, s, _ = proj.shape
    L = SSD_CHUNK
    xs_blk = (4 * DA_WIDTH + SSD_WIDTH) // SSD_WIDTH
    z_blk = 4 * DA_WIDTH // SSD_WIDTH
    bc_blk = (4 * DA_WIDTH + 2 * SSD_WIDTH) // SSD_BC
    return pl.pallas_call(
        _ssd_kernel,
        grid=(b, s // L),
        in_specs=[
            pl.BlockSpec((1, L, SSD_WIDTH), lambda bi, c: (bi, c, z_blk)),
            pl.BlockSpec((1, L, SSD_WIDTH), lambda bi, c: (bi, c, xs_blk)),
            pl.BlockSpec((1, L, SSD_BC), lambda bi, c: (bi, c, bc_blk)),
            pl.BlockSpec((1, L, LANES), lambda bi, c: (bi, c, 0)),
            _const_spec((SSD_CONV, SSD_WIDTH)),
            _const_spec((1, SSD_WIDTH)),
            _const_spec((SSD_CONV, SSD_BC)),
            _const_spec((1, SSD_BC)),
            _const_spec((1, LANES)),
            _const_spec((1, LANES)),
            _const_spec((1, SSD_WIDTH)),
            _const_spec((1, SSD_WIDTH)),
            _const_spec((LANES, SSD_WIDTH)),
        ],
        out_specs=pl.BlockSpec((1, L, SSD_WIDTH), lambda bi, c: (bi, c, 0)),
        out_shape=jax.ShapeDtypeStruct((b, s, SSD_WIDTH), BF16),
        scratch_shapes=[
            pltpu.VMEM((L + TAIL, SSD_WIDTH), F32),
            pltpu.VMEM((L + TAIL, SSD_BC), F32),
            pltpu.VMEM((TAIL, SSD_WIDTH), F32),
            pltpu.VMEM((TAIL, SSD_BC), F32),
            pltpu.VMEM((SSD_GROUPS, SSD_STATE, SSD_WIDTH // SSD_GROUPS), F32),
        ],
        compiler_params=_params("arbitrary", "arbitrary"),
        name="ssd",
    )(proj, proj, proj, dt_raw, cw_x, cb_x, cw_b, cb_b, dt_bias, a_log, d_skip_e, norm_w, expand)


MID_TM = 512


def _mid_proj_kernel(x_ref, ya_ref, yb_ref, wa_ref, wb_ref, nw_ref, win_ref, bin_ref, x1_ref, p_ref):
    x1 = (x_ref[...] + jnp.dot(ya_ref[...], wa_ref[...], preferred_element_type=F32)
          + jnp.dot(yb_ref[...], wb_ref[...], preferred_element_type=F32))
    x1_ref[...] = x1
    ms = jnp.mean(x1 * x1, axis=-1, keepdims=True)
    hn = (x1 * lax.rsqrt(ms + EPS) * nw_ref[...]).astype(BF16)
    p_ref[...] = (jnp.dot(hn, win_ref[...], preferred_element_type=F32) + bin_ref[...]).astype(BF16)


def _mid_proj(x2, ya, yb, w_a, w_b, norm_w, w_in, b_in):
    m = x2.shape[0]
    n = w_in.shape[1]
    tm = MID_TM
    row = lambda i: (i, 0)
    return pl.pallas_call(
        _mid_proj_kernel,
        grid=(m // tm,),
        in_specs=[
            pl.BlockSpec((tm, D_MODEL), row),
            pl.BlockSpec((tm, DA_WIDTH), row),
            pl.BlockSpec((tm, SSD_WIDTH), row),
            _const_spec((DA_WIDTH, D_MODEL)),
            _const_spec((SSD_WIDTH, D_MODEL)),
            _const_spec((1, D_MODEL)),
            _const_spec((D_MODEL, n)),
            _const_spec((1, n)),
        ],
        out_specs=[pl.BlockSpec((tm, D_MODEL), row), pl.BlockSpec((tm, n), row)],
        out_shape=[jax.ShapeDtypeStruct((m, D_MODEL), F32), jax.ShapeDtypeStruct((m, n), BF16)],
        compiler_params=_params("arbitrary"),
        name="mid_proj",
    )(x2, ya, yb, w_a, w_b, norm_w, w_in, b_in)


def _swa_out_kernel(sink_ref, q_ref, g_ref, kvp_ref, kvc_ref, x1_ref, wo_ref, fw_ref, o_ref, gated_ref):
    W = WINDOW
    nblk = pl.program_id(1)
    scale = SW_HEAD_DIM ** -0.5

    kv = jnp.concatenate([kvp_ref[0], kvc_ref[0]], axis=0).astype(F32)
    k2 = kv[:, 0:LANES]
    v2 = kv[:, LANES:2 * LANES]
    lane2 = lax.broadcasted_iota(jnp.int32, (2 * W, LANES), 1)
    low2 = lane2 < SW_HEAD_DIM
    k2r = pltpu.roll(k2, SW_HEAD_DIM, axis=1)
    v2r = pltpu.roll(v2, SW_HEAD_DIM, axis=1)
    kdup = [jnp.where(low2, k2, k2r).astype(BF16), jnp.where(low2, k2r, k2).astype(BF16)]
    vdup = [jnp.where(low2, v2, v2r).astype(BF16), jnp.where(low2, v2r, v2).astype(BF16)]

    qrel = lax.broadcasted_iota(jnp.int32, (W, 2 * W), 0) + W
    krel = lax.broadcasted_iota(jnp.int32, (W, 2 * W), 1)
    dist = qrel - krel
    valid = (dist >= 0) & (dist < W) & ((krel >= W) | (nblk > 0))
    distf = dist.astype(F32)
    lane = lax.broadcasted_iota(jnp.int32, (W, LANES), 1)
    low = lane < SW_HEAD_DIM

    for hp in range(SW_HEADS // 2):
        kvh = (2 * hp) // SW_GQ
        hs = slice(hp * LANES, (hp + 1) * LANES)
        q = q_ref[0, :, hs] * jnp.asarray(scale, BF16)
        zero = jnp.zeros_like(q)
        outs = []
        for half in range(2):
            h = 2 * hp + half
            qh = jnp.where(low, q, zero) if half == 0 else jnp.where(low, zero, q)
            s = lax.dot_general(qh, kdup[kvh], (((1,), (1,)), ((), ())), preferred_element_type=F32)
            s = jnp.where(valid, s - _alibi_slope(h, SW_HEADS) * distf, NEG)
            sink = sink_ref[0:1, h:h + 1]
            m = jnp.maximum(jnp.max(s, axis=-1, keepdims=True), sink)
            p = jnp.exp(s - m)
            denom = jnp.sum(p, axis=-1, keepdims=True) + jnp.exp(sink - m)
            o = jnp.dot(p.astype(BF16), vdup[kvh], preferred_element_type=F32)
            outs.append(o / denom)
        o_pair = jnp.where(low, outs[0], outs[1])
        g = g_ref[0, :, hs].astype(F32)
        gated_ref[:, hs] = (o_pair * _silu(g)).astype(BF16)

    x2 = x1_ref[0] + jnp.dot(gated_ref[...], wo_ref[...], preferred_element_type=F32)
    ms = jnp.mean(x2 * x2, axis=-1, keepdims=True)
    o_ref[0] = x2 * lax.rsqrt(ms + EPS) * fw_ref[...]


def _swa_out(proj_c, x1, sinks, w_out, final_w):
    b, s, _ = proj_c.shape
    W = WINDOW
    kv_blk = 2 * SW_WIDTH // (2 * LANES)
    return pl.pallas_call(
        _swa_out_kernel,
        grid=(b, s // W),
        in_specs=[
            _const_spec((1, LANES)),
            pl.BlockSpec((1, W, SW_WIDTH), lambda bi, n: (bi, n, 0)),
            pl.BlockSpec((1, W, SW_WIDTH), lambda bi, n: (bi, n, 1)),
            pl.BlockSpec((1, W, 2 * LANES), lambda bi, n: (bi, jnp.maximum(n - 1, 0), kv_blk)),
            pl.BlockSpec((1, W, 2 * LANES), lambda bi, n: (bi, n, kv_blk)),
            pl.BlockSpec((1, W, D_MODEL), lambda bi, n: (bi, n, 0)),
            _const_spec((SW_WIDTH, D_MODEL)),
            _const_spec((1, D_MODEL)),
        ],
        out_specs=pl.BlockSpec((1, W, D_MODEL), lambda bi, n: (bi, n, 0)),
        out_shape=jax.ShapeDtypeStruct((b, s, D_MODEL), F32),
        scratch_shapes=[pltpu.VMEM((W, SW_WIDTH), BF16)],
        compiler_params=_params("arbitrary", "arbitrary"),
        name="swa_out",
    )(sinks, proj_c, proj_c, proj_c, proj_c, x1, w_out, final_w)


def _even_layer(x, norm_w, w_in, conv_w, conv_b, dt_bias, a_log, d_skip, ssd_norm_w,
                lq1, lk1, lq2, lk2, subln_w, lambda_init):
    b, s, d = x.shape
    w_main = w_in[:, :EVEN_MAIN].astype(BF16)
    w_dt = jnp.pad(w_in[:, EVEN_MAIN:], ((0, 0), (0, LANES - SSD_HEADS))).astype(BF16)
    proj, dt_raw = _inproj_even(x.reshape(b * s, d), norm_w.reshape(1, d), w_main, w_dt)
    proj = proj.reshape(b, s, EVEN_MAIN)
    dt_raw = dt_raw.reshape(b, s, LANES)

    lam_params = jnp.stack([lq1, lk1, lq2, lk2]).astype(F32)
    y_a = _diff_attn(proj, lam_params, subln_w.reshape(1, -1).astype(F32), lambda_init)

    pad = lambda v: jnp.pad(v.astype(F32), (0, LANES - SSD_HEADS)).reshape(1, LANES)
    expand = (jnp.arange(LANES)[:, None] == (jnp.arange(SSD_WIDTH)[None, :] // SSD_HEAD_DIM)).astype(BF16)
    y_b = _ssd(proj, dt_raw,
               conv_w[:, :SSD_WIDTH], conv_b[:SSD_WIDTH].reshape(1, -1),
               conv_w[:, SSD_WIDTH:], conv_b[SSD_WIDTH:].reshape(1, -1),
               pad(dt_bias), pad(a_log),
               jnp.repeat(d_skip.astype(F32), SSD_HEAD_DIM).reshape(1, -1),
               ssd_norm_w.reshape(1, -1).astype(F32), expand)
    return y_a, y_b


def kernel(x, norm_a, w_in_a, conv_w_a, conv_b_a, dt_bias_a, a_log_a, d_skip_a, ssd_norm_a,
           lambda_q1_a, lambda_k1_a, lambda_q2_a, lambda_k2_a, subln_a, w_out_a,
           norm_c, w_in_c, b_in_c, sinks_c, w_out_c, final_norm):
    b, s, d = x.shape
    lambda_init = 0.8 - 0.6 * math.exp(-0.3 * 0)
    y_a, y_b = _even_layer(x, norm_a[0], w_in_a[0], conv_w_a[0], conv_b_a[0], dt_bias_a[0], a_log_a[0],
                           d_skip_a[0], ssd_norm_a[0], lambda_q1_a[0], lambda_k1_a[0], lambda_q2_a[0],
                           lambda_k2_a[0], subln_a[0], lambda_init)

    kv0 = SW_WIDTH
    g0 = SW_WIDTH + 2 * SW_KV_HEADS * SW_HEAD_DIM
    perm = lambda w: jnp.concatenate([w[..., :kv0], w[..., g0:], w[..., kv0:g0]], axis=-1)
    w_in = perm(w_in_c[0]).astype(BF16)
    b_in = perm(b_in_c[0]).reshape(1, -1).astype(F32)
    w_out = w_out_a[0].astype(BF16)
    x1, proj_c = _mid_proj(x.reshape(b * s, d), y_a.reshape(b * s, -1), y_b.reshape(b * s, -1),
                           w_out[:DA_WIDTH], w_out[DA_WIDTH:], norm_c[0].reshape(1, d), w_in, b_in)

    sinks = jnp.pad(sinks_c[0].astype(F32), (0, LANES - SW_HEADS)).reshape(1, LANES)
    return _swa_out(proj_c.reshape(b, s, -1), x1.reshape(b, s, d), sinks,
                    w_out_c[0].astype(BF16), final_norm.reshape(1, d))
```

```python
import functools
import math

import jax
import jax.numpy as jnp
from jax import lax
from jax.experimental import pallas as pl
from jax.experimental.pallas import tpu as pltpu

F32 = jnp.float32
BF16 = jnp.bfloat16

D_MODEL = 1024
EPS = 1e-5

DA_HEADS = 8
DA_HEAD_DIM = 64
DA_WIDTH = DA_HEADS * 2 * DA_HEAD_DIM

SSD_WIDTH = 1024
SSD_HEAD_DIM = 64
SSD_HEADS = SSD_WIDTH // SSD_HEAD_DIM
SSD_GROUPS = 2
SSD_STATE = 128
SSD_CONV = 4
SSD_CHUNK = 128
SSD_BC = 2 * SSD_GROUPS * SSD_STATE

SW_HEADS = 16
SW_KV_HEADS = 2
SW_GQ = SW_HEADS // SW_KV_HEADS
SW_HEAD_DIM = 64
SW_WIDTH = SW_HEADS * SW_HEAD_DIM
WINDOW = 128

EVEN_MAIN = 4 * DA_WIDTH + SSD_WIDTH + SSD_WIDTH + SSD_BC
LANES = 128
TAIL = 8
NEG = -1e30
LOG2E = math.log2(math.e)
DA_QSCALE = DA_HEAD_DIM ** -0.5 * LOG2E
SW_QSCALE = SW_HEAD_DIM ** -0.5 * LOG2E

VMEM_LIMIT = 56 * 1024 * 1024


def _alibi_slope(i, n):
    return float(2.0 ** (-8.0 * (i + 1) / n))


def _sigmoid(x):
    return 1.0 / (1.0 + jnp.exp(-x))


def _silu(x):
    return x * _sigmoid(x)


def _params(*sem):
    return pltpu.CompilerParams(dimension_semantics=sem, vmem_limit_bytes=VMEM_LIMIT)


def _const_spec(shape):
    nd = len(shape)
    return pl.BlockSpec(shape, lambda *_: (0,) * nd, pipeline_mode=pl.Buffered(1))


INPROJ_TM = 512
INPROJ_TN = 512


def _inproj_even_kernel(x_ref, nw_ref, w_ref, wdt_ref, o_ref, dt_ref, hn_ref):
    x = x_ref[...]
    ms = jnp.mean(x * x, axis=-1, keepdims=True)
    hn_ref[...] = (x * lax.rsqrt(ms + EPS) * nw_ref[...]).astype(BF16)
    n = o_ref.shape[1]
    for j in range(n // INPROJ_TN):
        sl = slice(j * INPROJ_TN, (j + 1) * INPROJ_TN)
        acc = jnp.dot(hn_ref[...], w_ref[:, sl], preferred_element_type=F32)
        if (j + 1) * INPROJ_TN <= DA_WIDTH:
            acc = acc * DA_QSCALE
        o_ref[:, sl] = acc.astype(BF16)
    dt_ref[...] = jnp.dot(hn_ref[...], wdt_ref[...], preferred_element_type=F32)


def _inproj_even(x2, norm_w, w_main, w_dt):
    m = x2.shape[0]
    n = w_main.shape[1]
    tm = INPROJ_TM
    return pl.pallas_call(
        _inproj_even_kernel,
        grid=(m // tm,),
        in_specs=[
            pl.BlockSpec((tm, D_MODEL), lambda i: (i, 0)),
            _const_spec((1, D_MODEL)),
            _const_spec((D_MODEL, n)),
            _const_spec((D_MODEL, LANES)),
        ],
        out_specs=[
            pl.BlockSpec((tm, n), lambda i: (i, 0)),
            pl.BlockSpec((tm, LANES), lambda i: (i, 0)),
        ],
        out_shape=[
            jax.ShapeDtypeStruct((m, n), BF16),
            jax.ShapeDtypeStruct((m, LANES), F32),
        ],
        scratch_shapes=[pltpu.VMEM((tm, D_MODEL), BF16)],
        compiler_params=_params("arbitrary"),
        name="inproj_even",
    )(x2, norm_w, w_main, w_dt)


DA_TQ = 256
DA_AHEAD = 2
N_SPLIT = 3


def _piece_select(x, lane):
    hi, mid, lo = _split3(x)
    zero = jnp.zeros_like(hi)
    return jnp.where(lane < 2, hi, jnp.where(lane < 4, mid, jnp.where(lane < 2 * N_SPLIT, lo, zero)))


def _diff_attn_kernel(lam_ref, sub_ref, q_ref, k_ref, v_ref, g_ref, o_ref,
                      qa_ref, kaug_ref, dmask_ref, m_ref, l_ref, acc_ref, sbuf_ref, *, lambda_init):
    tq = DA_TQ
    tk = DA_TQ
    qi = pl.program_id(1)

    lp = lam_ref[...]
    lam = (jnp.exp(jnp.sum(lp[0:1] * lp[1:2], axis=-1, keepdims=True))
           - jnp.exp(jnp.sum(lp[2:3] * lp[3:4], axis=-1, keepdims=True)) + lambda_init)

    lane = lax.broadcasted_iota(jnp.int32, (tq, LANES), 1)
    rowf = lax.broadcasted_iota(jnp.int32, (tq, LANES), 0).astype(F32)
    even = (lane & 1) == 0

    @pl.when(qi == 0)
    def _():
        krow = lax.broadcasted_iota(jnp.int32, (tk, 2 * tq), 0)
        qcol = lax.broadcasted_iota(jnp.int32, (tk, 2 * tq), 1)
        qcol = jnp.where(qcol >= tq, qcol - tq, qcol)
        dmask_ref[...] = jnp.where(krow <= qcol, 0.0, NEG)
        kext = jnp.where(lane < 2 * N_SPLIT, jnp.where(even, rowf, 1.0), 0.0).astype(BF16)
        qext = _piece_select(jnp.where(even, LOG2E, -LOG2E * rowf), lane)
        for h in range(DA_HEADS):
            qext_h = qext * jnp.asarray(_alibi_slope(h, DA_HEADS), BF16)
            qa_ref[h, 0:tq, LANES:2 * LANES] = qext_h
            qa_ref[h, tq:2 * tq, LANES:2 * LANES] = qext_h
        for t in range(k_ref.shape[1] // tk):
            rows = slice(t * tk, (t + 1) * tk)
            for h in range(DA_HEADS):
                kaug_ref[rows, 2 * h * LANES:(2 * h + 1) * LANES] = k_ref[0, rows, h * LANES:(h + 1) * LANES]
                kaug_ref[rows, (2 * h + 1) * LANES:(2 * h + 2) * LANES] = kext

    first_map = lane < DA_HEAD_DIM
    for h in range(DA_HEADS):
        q = q_ref[0, :, h * LANES:(h + 1) * LANES]
        zero = jnp.zeros_like(q)
        qa_ref[h, 0:tq, 0:LANES] = jnp.where(first_map, q, zero)
        qa_ref[h, tq:2 * tq, 0:LANES] = jnp.where(first_map, zero, q)

    m_ref[...] = jnp.full_like(m_ref, NEG)
    l_ref[...] = jnp.zeros_like(l_ref)
    acc_ref[...] = jnp.zeros_like(acc_ref)

    def scores(j, h):
        kst = pl.multiple_of(j * tk, tk)
        ka = kaug_ref[pl.ds(kst, tk), 2 * h * LANES:(2 * h + 2) * LANES]
        return lax.dot_general(ka, qa_ref[h], (((1,), (1,)), ((), ())), preferred_element_type=F32)

    def tile(j, diagonal):
        kst = pl.multiple_of(j * tk, tk)
        j_next = jnp.maximum(j - 1, 0)
        pending = [sbuf_ref[i] for i in range(DA_AHEAD)]
        for h in range(DA_HEADS):
            hs = slice(h * LANES, (h + 1) * LANES)
            s = pending.pop(0)
            if h + DA_AHEAD < DA_HEADS:
                pending.append(scores(j, h + DA_AHEAD))
            else:
                sbuf_ref[h + DA_AHEAD - DA_HEADS] = scores(j_next, h + DA_AHEAD - DA_HEADS)
            v = v_ref[0, pl.ds(kst, tk), hs]
            if diagonal:
                s = dmask_ref[...] + s
            tmax = jnp.max(s, axis=0, keepdims=True)
            m_old = m_ref[h] + _alibi_slope(h, DA_HEADS) * LOG2E * tk
            m_new = jnp.maximum(m_old, tmax)
            alpha = jnp.exp2(m_old - m_new)
            p = jnp.exp2(s - m_new)
            l_ref[h] = alpha * l_ref[h] + jnp.sum(p, axis=0, keepdims=True)
            pv = lax.dot_general(v, p.astype(BF16), (((0,), (0,)), ((), ())),
                                 preferred_element_type=F32)
            acc_ref[h] = alpha * acc_ref[h] + pv
            m_ref[h] = m_new

    for i in range(DA_AHEAD):
        sbuf_ref[i] = scores(qi, i)

    def diag_body(j, carry):
        tile(j, True)
        return carry

    lax.fori_loop(qi, qi + 1, diag_body, 0)

    def body(i, carry):
        tile(qi - 1 - i, False)
        return carry

    lax.fori_loop(0, qi, body, 0)

    post = sub_ref[...] * (1.0 - lambda_init)
    for h in range(DA_HEADS):
        hs = slice(h * LANES, (h + 1) * LANES)
        o = acc_ref[h] * (1.0 / l_ref[h])
        attn = o[:, 0:tq] - lam * o[:, tq:2 * tq]
        y = attn * lax.rsqrt(jnp.mean(attn * attn, axis=0, keepdims=True) + EPS)
        g = g_ref[0, :, hs].astype(F32)
        o_ref[0, :, hs] = (y.T * post * _silu(g)).astype(BF16)


def _diff_attn(proj, lam_params, subln_w, lambda_init):
    b, s, _ = proj.shape
    tq = DA_TQ
    kern = functools.partial(_diff_attn_kernel, lambda_init=lambda_init)
    return pl.pallas_call(
        kern,
        grid=(b, s // tq),
        in_specs=[
            _const_spec((4, DA_HEAD_DIM)),
            _const_spec((1, 2 * DA_HEAD_DIM)),
            pl.BlockSpec((1, tq, DA_WIDTH), lambda bi, qi: (bi, qi, 0)),
            pl.BlockSpec((1, s, DA_WIDTH), lambda bi, qi: (bi, 0, 1)),
            pl.BlockSpec((1, s, DA_WIDTH), lambda bi, qi: (bi, 0, 2)),
            pl.BlockSpec((1, tq, DA_WIDTH), lambda bi, qi: (bi, qi, 3)),
        ],
        out_specs=pl.BlockSpec((1, tq, DA_WIDTH), lambda bi, qi: (bi, qi, 0)),
        out_shape=jax.ShapeDtypeStruct((b, s, DA_WIDTH), BF16),
        scratch_shapes=[
            pltpu.VMEM((DA_HEADS, 2 * tq, 2 * LANES), BF16),
            pltpu.VMEM((s, 2 * DA_WIDTH), BF16),
            pltpu.VMEM((tq, 2 * tq), F32),
            pltpu.VMEM((DA_HEADS, 1, 2 * tq), F32),
            pltpu.VMEM((DA_HEADS, 1, 2 * tq), F32),
            pltpu.VMEM((DA_HEADS, LANES, 2 * tq), F32),
            pltpu.VMEM((DA_AHEAD, tq, 2 * tq), F32),
        ],
        compiler_params=_params("arbitrary", "arbitrary"),
        name="diff_attn",
    )(lam_params, subln_w, proj, proj, proj, proj)


def _split3(x):
    hi = x.astype(BF16)
    r1 = x - hi.astype(F32)
    mid = r1.astype(BF16)
    lo = (r1 - mid.astype(F32)).astype(BF16)
    return hi, mid, lo


def _expand_heads(x, r_ref):
    r = r_ref[...]
    hi, mid, lo = _split3(x)
    return (jnp.dot(hi, r, preferred_element_type=F32) + jnp.dot(mid, r, preferred_element_type=F32)
            + jnp.dot(lo, r, preferred_element_type=F32))


def _conv_silu(raw, ext_ref, tail_ref, w_ref, b_ref):
    rows = raw.shape[0]
    ext_ref[0:TAIL, :] = tail_ref[...]
    ext_ref[TAIL:TAIL + rows, :] = raw
    tail_ref[...] = raw[rows - TAIL:rows, :]
    acc = b_ref[...] + w_ref[0:1, :] * ext_ref[TAIL - 3:TAIL - 3 + rows, :]
    for k in range(1, SSD_CONV):
        acc = acc + w_ref[k:k + 1, :] * ext_ref[TAIL - 3 + k:TAIL - 3 + k + rows, :]
    return _silu(acc)


def _ssd_kernel(z_ref, xs_ref, bc_ref, dt_ref, cwx_ref, cbx_ref, cwb_ref, cbb_ref, dtb_ref, alog_ref,
                dskip_ref, nw_ref, r_ref, o_ref,
                extx_ref, extb_ref, tailx_ref, tailb_ref, state_ref):
    L = SSD_CHUNK
    hpg = SSD_HEADS // SSD_GROUPS
    gw = SSD_WIDTH // SSD_GROUPS

    @pl.when(pl.program_id(1) == 0)
    def _():
        tailx_ref[...] = jnp.zeros_like(tailx_ref)
        tailb_ref[...] = jnp.zeros_like(tailb_ref)
        state_ref[...] = jnp.zeros_like(state_ref)

    xs = _conv_silu(xs_ref[0].astype(F32), extx_ref, tailx_ref, cwx_ref, cbx_ref)
    bc = _conv_silu(bc_ref[0].astype(F32), extb_ref, tailb_ref, cwb_ref, cbb_ref)
    bcb = bc.astype(BF16)

    draw = dt_ref[0] + dtb_ref[...]
    dt = jnp.maximum(draw, 0.0) + jnp.log1p(jnp.exp(-jnp.abs(draw)))
    a = -jnp.exp(alog_ref[...])
    da = dt * a

    row = lax.broadcasted_iota(jnp.int32, (L, L), 0)
    col = lax.broadcasted_iota(jnp.int32, (L, L), 1)
    tril = row >= col
    hi, mid, lo = _split3(da)
    ones = tril.astype(BF16)
    cs = (jnp.dot(ones, hi, preferred_element_type=F32) + jnp.dot(ones, mid, preferred_element_type=F32)
          + jnp.dot(ones, lo, preferred_element_type=F32))
    cs_t = cs.T
    cs_last = cs[L - 1:L, :]

    dt_e = _expand_heads(dt, r_ref)
    cs_e = _expand_heads(cs, r_ref)
    cs_last_e = cs_e[L - 1:L, :]
    xdt = xs * dt_e
    xdt_b = xdt.astype(BF16)
    w_state = (xdt * jnp.exp(cs_last_e - cs_e)).astype(BF16)
    e_in = jnp.exp(cs_e)
    chunk_decay = jnp.exp(cs_last_e)

    lane = lax.broadcasted_iota(jnp.int32, (L, LANES), 1)
    first_head = lane < SSD_HEAD_DIM
    zeros_b = jnp.zeros((L, LANES), BF16)

    y_parts = []
    for g in range(SSD_GROUPS):
        bg = bcb[:, g * SSD_STATE:(g + 1) * SSD_STATE]
        cg = bcb[:, (SSD_GROUPS + g) * SSD_STATE:(SSD_GROUPS + g + 1) * SSD_STATE]
        cb = lax.dot_general(cg, bg, (((1,), (1,)), ((), ())), preferred_element_type=F32)
        for jp in range(hpg // 2):
            pair = None
            for half in range(2):
                hh = g * hpg + 2 * jp + half
                seg = cs[:, hh:hh + 1] - cs_t[hh:hh + 1, :]
                lm = jnp.exp(jnp.where(tril, seg, NEG))
                mj = (cb * lm).astype(BF16)
                xp = xdt_b[:, (g * hpg + 2 * jp) * SSD_HEAD_DIM:(g * hpg + 2 * jp + 2) * SSD_HEAD_DIM]
                xp = jnp.where(first_head, xp, zeros_b) if half == 0 else jnp.where(first_head, zeros_b, xp)
                t = jnp.dot(mj, xp, preferred_element_type=F32)
                pair = t if pair is None else pair + t
            y_parts.append(pair)
        gs = slice(g * gw, (g + 1) * gw)
        prev = state_ref[g]
        y_off = jnp.dot(cg, prev.astype(BF16), preferred_element_type=F32) * e_in[:, gs]
        st = lax.dot_general(bg, w_state[:, gs], (((0,), (0,)), ((), ())), preferred_element_type=F32)
        state_ref[g] = prev * chunk_decay[:, gs] + st
        y_parts.append(y_off)

    nq = hpg // 2
    y_diag = jnp.concatenate(y_parts[0:nq] + y_parts[nq + 1:2 * nq + 1], axis=-1)
    y_off = jnp.concatenate([y_parts[nq], y_parts[2 * nq + 1]], axis=-1)
    y = y_diag + y_off + xs * dskip_ref[...]
    y = y * _silu(z_ref[0].astype(F32))
    outs = []
    for g in range(SSD_GROUPS):
        yg = y[:, g * gw:(g + 1) * gw]
        outs.append(yg * lax.rsqrt(jnp.mean(yg * yg, axis=-1, keepdims=True) + EPS))
    o_ref[0] = (jnp.concatenate(outs, axis=-1) * nw_ref[...]).astype(BF16)


def _ssd(proj, dt_raw, cw_x, cb_x, cw_b, cb_b, dt_bias, a_log, d_skip_e, norm_w, expand):
    b, s, _ = proj.shape
    L = SSD_CHUNK
    xs_blk = (4 * DA_WIDTH + SSD_WIDTH) // SSD_WIDTH
    z_blk = 4 * DA_WIDTH // SSD_WIDTH
    bc_blk = (4 * DA_WIDTH + 2 * SSD_WIDTH) // SSD_BC
    return pl.pallas_call(
        _ssd_kernel,
        grid=(b, s // L),
        in_specs=[
            pl.BlockSpec((1, L, SSD_WIDTH), lambda bi, c: (bi, c, z_blk)),
            pl.BlockSpec((1, L, SSD_WIDTH), lambda bi, c: (bi, c, xs_blk)),
            pl.BlockSpec((1, L, SSD_BC), lambda bi, c: (bi, c, bc_blk)),
            pl.BlockSpec((1, L, LANES), lambda bi, c: (bi, c, 0)),
            _const_spec((SSD_CONV, SSD_WIDTH)),
            _const_spec((1, SSD_WIDTH)),
            _const_spec((SSD_CONV, SSD_BC)),
            _const_spec((1, SSD_BC)),
            _const_spec((1, LANES)),
            _const_spec((1, LANES)),
            _const_spec((1, SSD_WIDTH)),
            _const_spec((1, SSD_WIDTH)),
            _const_spec((LANES, SSD_WIDTH)),
        ],
        out_specs=pl.BlockSpec((1, L, SSD_WIDTH), lambda bi, c: (bi, c, 0)),
        out_shape=jax.ShapeDtypeStruct((b, s, SSD_WIDTH), BF16),
        scratch_shapes=[
            pltpu.VMEM((L + TAIL, SSD_WIDTH), F32),
            pltpu.VMEM((L + TAIL, SSD_BC), F32),
            pltpu.VMEM((TAIL, SSD_WIDTH), F32),
            pltpu.VMEM((TAIL, SSD_BC), F32),
            pltpu.VMEM((SSD_GROUPS, SSD_STATE, SSD_WIDTH // SSD_GROUPS), F32),
        ],
        compiler_params=_params("arbitrary", "arbitrary"),
        name="ssd",
    )(proj, proj, proj, dt_raw, cw_x, cb_x, cw_b, cb_b, dt_bias, a_log, d_skip_e, norm_w, expand)


MID_TM = 512


def _mid_proj_kernel(x_ref, ya_ref, yb_ref, wa_ref, wb_ref, nw_ref, win_ref, bin_ref, x1_ref, p_ref):
    x1 = (x_ref[...] + jnp.dot(ya_ref[...], wa_ref[...], preferred_element_type=F32)
          + jnp.dot(yb_ref[...], wb_ref[...], preferred_element_type=F32))
    x1_ref[...] = x1
    ms = jnp.mean(x1 * x1, axis=-1, keepdims=True)
    hn = (x1 * lax.rsqrt(ms + EPS) * nw_ref[...]).astype(BF16)
    nq = SW_WIDTH
    p_ref[:, 0:nq] = ((jnp.dot(hn, win_ref[:, 0:nq], preferred_element_type=F32) + bin_ref[:, 0:nq])
                      * SW_QSCALE).astype(BF16)
    p_ref[:, nq:] = (jnp.dot(hn, win_ref[:, nq:], preferred_element_type=F32) + bin_ref[:, nq:]).astype(BF16)


def _mid_proj(x2, ya, yb, w_a, w_b, norm_w, w_in, b_in):
    m = x2.shape[0]
    n = w_in.shape[1]
    tm = MID_TM
    row = lambda i: (i, 0)
    return pl.pallas_call(
        _mid_proj_kernel,
        grid=(m // tm,),
        in_specs=[
            pl.BlockSpec((tm, D_MODEL), row),
            pl.BlockSpec((tm, DA_WIDTH), row),
            pl.BlockSpec((tm, SSD_WIDTH), row),
            _const_spec((DA_WIDTH, D_MODEL)),
            _const_spec((SSD_WIDTH, D_MODEL)),
            _const_spec((1, D_MODEL)),
            _const_spec((D_MODEL, n)),
            _const_spec((1, n)),
        ],
        out_specs=[pl.BlockSpec((tm, D_MODEL), row), pl.BlockSpec((tm, n), row)],
        out_shape=[jax.ShapeDtypeStruct((m, D_MODEL), F32), jax.ShapeDtypeStruct((m, n), BF16)],
        compiler_params=_params("arbitrary"),
        name="mid_proj",
    )(x2, ya, yb, w_a, w_b, norm_w, w_in, b_in)


def _swa_out_kernel(sink_ref, q_ref, g_ref, kvp_ref, kvc_ref, x1_ref, wo_ref, fw_ref, o_ref,
                    qq_ref, ka_ref, mask_ref, gated_ref):
    W = WINDOW
    nblk = pl.program_id(1)
    hpg = SW_GQ

    lane = lax.broadcasted_iota(jnp.int32, (W, LANES), 1)
    even = (lane & 1) == 0
    low = lane < SW_HEAD_DIM

    @pl.when(nblk == 0)
    def _():
        krel = lax.broadcasted_iota(jnp.int32, (2 * W, W), 0)
        qrel = lax.broadcasted_iota(jnp.int32, (2 * W, W), 1) + W
        dist = qrel - krel
        band = (dist >= 0) & (dist < W)
        mask_ref[0] = jnp.where(band, 0.0, NEG)
        mask_ref[1] = jnp.where(band & (krel >= W), 0.0, NEG)
        lane2 = lax.broadcasted_iota(jnp.int32, (2 * W, LANES), 1)
        krow = lax.broadcasted_iota(jnp.int32, (2 * W, LANES), 0).astype(F32)
        kext = jnp.where(lane2 < 2 * N_SPLIT, jnp.where((lane2 & 1) == 0, krow, 1.0), 0.0).astype(BF16)
        qrow = lax.broadcasted_iota(jnp.int32, (W, LANES), 0).astype(F32) + float(W)
        for g in range(SW_KV_HEADS):
            ka_ref[g, :, LANES:2 * LANES] = kext
            for hl in range(hpg):
                slope2 = _alibi_slope(g * hpg + hl, SW_HEADS) * LOG2E
                qq_ref[g, hl * W:(hl + 1) * W, LANES:2 * LANES] = _piece_select(
                    jnp.where(even, slope2, -slope2 * qrow), lane)

    kv = jnp.concatenate([kvp_ref[0], kvc_ref[0]], axis=0)
    k2 = kv[:, 0:LANES].astype(F32)
    vblk = kv[:, LANES:2 * LANES]
    k2r = pltpu.roll(k2, SW_HEAD_DIM, axis=1)
    low2 = lax.broadcasted_iota(jnp.int32, (2 * W, LANES), 1) < SW_HEAD_DIM
    ka_ref[0, :, 0:LANES] = jnp.where(low2, k2, k2r).astype(BF16)
    ka_ref[1, :, 0:LANES] = jnp.where(low2, k2r, k2).astype(BF16)

    for hp in range(SW_HEADS // 2):
        g, hl = (2 * hp) // hpg, (2 * hp) % hpg
        q = q_ref[0, :, hp * LANES:(hp + 1) * LANES]
        zero = jnp.zeros_like(q)
        qq_ref[g, hl * W:(hl + 1) * W, 0:LANES] = jnp.where(low, q, zero)
        qq_ref[g, (hl + 1) * W:(hl + 2) * W, 0:LANES] = jnp.where(low, zero, q)

    mask = mask_ref[jnp.where(nblk == 0, 1, 0)]
    scores = [lax.dot_general(ka_ref[g], qq_ref[g], (((1,), (1,)), ((), ())), preferred_element_type=F32)
              for g in range(SW_KV_HEADS)]
    for g in range(SW_KV_HEADS):
        probs, inv = [], []
        for hl in range(hpg):
            h = g * hpg + hl
            s = mask + scores[g][:, hl * W:(hl + 1) * W]
            sink2 = sink_ref[0:1, h:h + 1] * LOG2E
            m = jnp.maximum(jnp.max(s, axis=0, keepdims=True), sink2)
            p = jnp.exp2(s - m)
            inv.append(1.0 / (jnp.sum(p, axis=0, keepdims=True) + jnp.exp2(sink2 - m)))
            probs.append(p.astype(BF16))
        pt = jnp.concatenate(probs, axis=1)
        ot = lax.dot_general(vblk, pt, (((0,), (0,)), ((), ())), preferred_element_type=F32)
        ot = ot[g * SW_HEAD_DIM:(g + 1) * SW_HEAD_DIM, :] * jnp.concatenate(inv, axis=1)
        for i in range(hpg // 2):
            pair = jnp.concatenate([ot[:, (2 * i) * W:(2 * i + 1) * W],
                                    ot[:, (2 * i + 1) * W:(2 * i + 2) * W]], axis=0)
            hs = slice((g * hpg // 2 + i) * LANES, (g * hpg // 2 + i + 1) * LANES)
            gate = g_ref[0, :, hs].astype(F32)
            gated_ref[:, hs] = (pair.T * _silu(gate)).astype(BF16)

    x2 = x1_ref[0] + jnp.dot(gated_ref[...], wo_ref[...], preferred_element_type=F32)
    ms = jnp.mean(x2 * x2, axis=-1, keepdims=True)
    o_ref[0] = x2 * lax.rsqrt(ms + EPS) * fw_ref[...]


def _swa_out(proj_c, x1, sinks, w_out, final_w):
    b, s, _ = proj_c.shape
    W = WINDOW
    kv_blk = 2 * SW_WIDTH // (2 * LANES)
    return pl.pallas_call(
        _swa_out_kernel,
        grid=(b, s // W),
        in_specs=[
            _const_spec((1, LANES)),
            pl.BlockSpec((1, W, SW_WIDTH), lambda bi, n: (bi, n, 0)),
            pl.BlockSpec((1, W, SW_WIDTH), lambda bi, n: (bi, n, 1)),
            pl.BlockSpec((1, W, 2 * LANES), lambda bi, n: (bi, jnp.maximum(n - 1, 0), kv_blk)),
            pl.BlockSpec((1, W, 2 * LANES), lambda bi, n: (bi, n, kv_blk)),
            pl.BlockSpec((1, W, D_MODEL), lambda bi, n: (bi, n, 0)),
            _const_spec((SW_WIDTH, D_MODEL)),
            _const_spec((1, D_MODEL)),
        ],
        out_specs=pl.BlockSpec((1, W, D_MODEL), lambda bi, n: (bi, n, 0)),
        out_shape=jax.ShapeDtypeStruct((b, s, D_MODEL), F32),
        scratch_shapes=[
            pltpu.VMEM((SW_KV_HEADS, SW_GQ * W, 2 * LANES), BF16),
            pltpu.VMEM((SW_KV_HEADS, 2 * W, 2 * LANES), BF16),
            pltpu.VMEM((2, 2 * W, W), F32),
            pltpu.VMEM((W, SW_WIDTH), BF16),
        ],
        compiler_params=_params("arbitrary", "arbitrary"),
        name="swa_out",
    )(sinks, proj_c, proj_c, proj_c, proj_c, x1, w_out, final_w)


def _even_layer(x, norm_w, w_in, conv_w, conv_b, dt_bias, a_log, d_skip, ssd_norm_w,
                lq1, lk1, lq2, lk2, subln_w, lambda_init):
    b, s, d = x.shape
    w_main = w_in[:, :EVEN_MAIN].astype(BF16)
    w_dt = jnp.pad(w_in[:, EVEN_MAIN:], ((0, 0), (0, LANES - SSD_HEADS))).astype(BF16)
    proj, dt_raw = _inproj_even(x.reshape(b * s, d), norm_w.reshape(1, d), w_main, w_dt)
    proj = proj.reshape(b, s, EVEN_MAIN)
    dt_raw = dt_raw.reshape(b, s, LANES)

    lam_params = jnp.stack([lq1, lk1, lq2, lk2]).astype(F32)
    y_a = _diff_attn(proj, lam_params, subln_w.reshape(1, -1).astype(F32), lambda_init)

    pad = lambda v: jnp.pad(v.astype(F32), (0, LANES - SSD_HEADS)).reshape(1, LANES)
    expand = (jnp.arange(LANES)[:, None] == (jnp.arange(SSD_WIDTH)[None, :] // SSD_HEAD_DIM)).astype(BF16)
    y_b = _ssd(proj, dt_raw,
               conv_w[:, :SSD_WIDTH], conv_b[:SSD_WIDTH].reshape(1, -1),
               conv_w[:, SSD_WIDTH:], conv_b[SSD_WIDTH:].reshape(1, -1),
               pad(dt_bias), pad(a_log),
               jnp.repeat(d_skip.astype(F32), SSD_HEAD_DIM).reshape(1, -1),
               ssd_norm_w.reshape(1, -1).astype(F32), expand)
    return y_a, y_b


def kernel(x, norm_a, w_in_a, conv_w_a, conv_b_a, dt_bias_a, a_log_a, d_skip_a, ssd_norm_a,
           lambda_q1_a, lambda_k1_a, lambda_q2_a, lambda_k2_a, subln_a, w_out_a,
           norm_c, w_in_c, b_in_c, sinks_c, w_out_c, final_norm):
    b, s, d = x.shape
    lambda_init = 0.8 - 0.6 * math.exp(-0.3 * 0)
    y_a, y_b = _even_layer(x, norm_a[0], w_in_a[0], conv_w_a[0], conv_b_a[0], dt_bias_a[0], a_log_a[0],
                           d_skip_a[0], ssd_norm_a[0], lambda_q1_a[0], lambda_k1_a[0], lambda_q2_a[0],
                           lambda_k2_a[0], subln_a[0], lambda_init)

    kv0 = SW_WIDTH
    g0 = SW_WIDTH + 2 * SW_KV_HEADS * SW_HEAD_DIM
    perm = lambda w: jnp.concatenate([w[..., :kv0], w[..., g0:], w[..., kv0:g0]], axis=-1)
    w_in = perm(w_in_c[0]).astype(BF16)
    b_in = perm(b_in_c[0]).reshape(1, -1).astype(F32)
    w_out = w_out_a[0].astype(BF16)
    x1, proj_c = _mid_proj(x.reshape(b * s, d), y_a.reshape(b * s, -1), y_b.reshape(b * s, -1),
                           w_out[:DA_WIDTH], w_out[DA_WIDTH:], norm_c[0].reshape(1, d), w_in, b_in)

    sinks = jnp.pad(sinks_c[0].astype(F32), (0, LANES - SW_HEADS)).reshape(1, LANES)
    return _swa_out(proj_c.reshape(b, s, -1), x1.reshape(b, s, d), sinks,
                    w_out_c[0].astype(BF16), final_norm.reshape(1, d))
```

```python
import functools
import math

import jax
import jax.numpy as jnp
from jax import lax
from jax.experimental import pallas as pl
from jax.experimental.pallas import tpu as pltpu

F32 = jnp.float32
BF16 = jnp.bfloat16

D_MODEL = 1024
EPS = 1e-5

DA_HEADS = 8
DA_HEAD_DIM = 64
DA_WIDTH = DA_HEADS * 2 * DA_HEAD_DIM

SSD_WIDTH = 1024
SSD_HEAD_DIM = 64
SSD_HEADS = SSD_WIDTH // SSD_HEAD_DIM
SSD_GROUPS = 2
SSD_STATE = 128
SSD_CONV = 4
SSD_CHUNK = 128
SSD_BC = 2 * SSD_GROUPS * SSD_STATE

SW_HEADS = 16
SW_KV_HEADS = 2
SW_GQ = SW_HEADS // SW_KV_HEADS
SW_HEAD_DIM = 64
SW_WIDTH = SW_HEADS * SW_HEAD_DIM
WINDOW = 128

EVEN_MAIN = 4 * DA_WIDTH + SSD_WIDTH + SSD_WIDTH + SSD_BC
LANES = 128
TAIL = 8
NEG = -1e30
LOG2E = math.log2(math.e)
DA_QSCALE = DA_HEAD_DIM ** -0.5 * LOG2E
SW_QSCALE = SW_HEAD_DIM ** -0.5 * LOG2E

VMEM_LIMIT = 56 * 1024 * 1024


def _alibi_slope(i, n):
    return float(2.0 ** (-8.0 * (i + 1) / n))


def _sigmoid(x):
    return 1.0 / (1.0 + jnp.exp(-x))


def _silu(x):
    return x * _sigmoid(x)


def _params(*sem):
    return pltpu.CompilerParams(dimension_semantics=sem, vmem_limit_bytes=VMEM_LIMIT)


def _const_spec(shape):
    nd = len(shape)
    return pl.BlockSpec(shape, lambda *_: (0,) * nd, pipeline_mode=pl.Buffered(1))


INPROJ_TM = 512
INPROJ_TN = 512


def _inproj_even_kernel(x_ref, nw_ref, w_ref, wdt_ref, o_ref, dt_ref, hn_ref):
    x = x_ref[...]
    ms = jnp.mean(x * x, axis=-1, keepdims=True)
    hn_ref[...] = (x * lax.rsqrt(ms + EPS) * nw_ref[...]).astype(BF16)
    n = o_ref.shape[1]
    for j in range(n // INPROJ_TN):
        sl = slice(j * INPROJ_TN, (j + 1) * INPROJ_TN)
        acc = jnp.dot(hn_ref[...], w_ref[:, sl], preferred_element_type=F32)
        if (j + 1) * INPROJ_TN <= DA_WIDTH:
            acc = acc * DA_QSCALE
        o_ref[:, sl] = acc.astype(BF16)
    dt_ref[...] = jnp.dot(hn_ref[...], wdt_ref[...], preferred_element_type=F32)


def _inproj_even(x2, norm_w, w_main, w_dt):
    m = x2.shape[0]
    n = w_main.shape[1]
    tm = INPROJ_TM
    return pl.pallas_call(
        _inproj_even_kernel,
        grid=(m // tm,),
        in_specs=[
            pl.BlockSpec((tm, D_MODEL), lambda i: (i, 0)),
            _const_spec((1, D_MODEL)),
            _const_spec((D_MODEL, n)),
            _const_spec((D_MODEL, LANES)),
        ],
        out_specs=[
            pl.BlockSpec((tm, n), lambda i: (i, 0)),
            pl.BlockSpec((tm, LANES), lambda i: (i, 0)),
        ],
        out_shape=[
            jax.ShapeDtypeStruct((m, n), BF16),
            jax.ShapeDtypeStruct((m, LANES), F32),
        ],
        scratch_shapes=[pltpu.VMEM((tm, D_MODEL), BF16)],
        compiler_params=_params("arbitrary"),
        name="inproj_even",
    )(x2, norm_w, w_main, w_dt)


DA_TQ = 256
DA_AHEAD = 2
N_SPLIT = 3


def _piece_select(x, lane):
    hi, mid, lo = _split3(x)
    zero = jnp.zeros_like(hi)
    return jnp.where(lane < 2, hi, jnp.where(lane < 4, mid, jnp.where(lane < 2 * N_SPLIT, lo, zero)))


def _diff_attn_kernel(lam_ref, sub_ref, q_ref, k_ref, v_ref, g_ref, o_ref,
                      qa_ref, kaug_ref, dmask_ref, m_ref, l_ref, acc_ref, sbuf_ref, *, lambda_init):
    tq = DA_TQ
    tk = DA_TQ
    qi = pl.program_id(1)

    lp = lam_ref[...]
    lam = (jnp.exp(jnp.sum(lp[0:1] * lp[1:2], axis=-1, keepdims=True))
           - jnp.exp(jnp.sum(lp[2:3] * lp[3:4], axis=-1, keepdims=True)) + lambda_init)

    lane = lax.broadcasted_iota(jnp.int32, (tq, LANES), 1)
    rowf = lax.broadcasted_iota(jnp.int32, (tq, LANES), 0).astype(F32)
    even = (lane & 1) == 0

    @pl.when(qi == 0)
    def _():
        krow = lax.broadcasted_iota(jnp.int32, (tk, 2 * tq), 0)
        qcol = lax.broadcasted_iota(jnp.int32, (tk, 2 * tq), 1)
        qcol = jnp.where(qcol >= tq, qcol - tq, qcol)
        dmask_ref[...] = jnp.where(krow <= qcol, 0.0, NEG)
        kext = jnp.where(lane < 2 * N_SPLIT, jnp.where(even, rowf, 1.0), 0.0).astype(BF16)
        qext_t = _piece_select(jnp.where(even, LOG2E, -LOG2E * rowf), lane).astype(F32).T
        for h in range(DA_HEADS):
            qext_h = (qext_t * _alibi_slope(h, DA_HEADS)).astype(BF16)
            qa_ref[h, LANES:2 * LANES, 0:tq] = qext_h
            qa_ref[h, LANES:2 * LANES, tq:2 * tq] = qext_h
        for t in range(k_ref.shape[1] // tk):
            rows = slice(t * tk, (t + 1) * tk)
            for h in range(DA_HEADS):
                kaug_ref[rows, 2 * h * LANES:(2 * h + 1) * LANES] = k_ref[0, rows, h * LANES:(h + 1) * LANES]
                kaug_ref[rows, (2 * h + 1) * LANES:(2 * h + 2) * LANES] = kext

    first_map = lax.broadcasted_iota(jnp.int32, (LANES, tq), 0) < DA_HEAD_DIM
    for h in range(DA_HEADS):
        qt = q_ref[0, :, h * LANES:(h + 1) * LANES].astype(F32).T
        qa_ref[h, 0:LANES, 0:tq] = jnp.where(first_map, qt, 0.0).astype(BF16)
        qa_ref[h, 0:LANES, tq:2 * tq] = jnp.where(first_map, 0.0, qt).astype(BF16)

    def scores(j, h):
        kst = pl.multiple_of(j * tk, tk)
        ka = kaug_ref[pl.ds(kst, tk), 2 * h * LANES:(2 * h + 2) * LANES]
        return jnp.dot(ka, qa_ref[h], preferred_element_type=F32)

    def tile(j, diagonal):
        kst = pl.multiple_of(j * tk, tk)
        j_next = jnp.maximum(j - 1, 0)
        pending = [sbuf_ref[i] for i in range(DA_AHEAD)]
        for h in range(DA_HEADS):
            hs = slice(h * LANES, (h + 1) * LANES)
            s = pending.pop(0)
            if h + DA_AHEAD < DA_HEADS:
                pending.append(scores(j, h + DA_AHEAD))
            else:
                sbuf_ref[h + DA_AHEAD - DA_HEADS] = scores(j_next, h + DA_AHEAD - DA_HEADS)
            v = v_ref[0, pl.ds(kst, tk), hs]
            if diagonal:
                s = dmask_ref[...] + s
            tmax = jnp.max(s, axis=0, keepdims=True)
            if diagonal:
                m_new = tmax
                p = jnp.exp2(s - m_new)
                l_ref[h] = jnp.sum(p, axis=0, keepdims=True)
                pv = lax.dot_general(v, p.astype(BF16), (((0,), (0,)), ((), ())),
                                     preferred_element_type=F32)
                acc_ref[h] = pv
            else:
                m_old = m_ref[h] + _alibi_slope(h, DA_HEADS) * LOG2E * tk
                m_new = jnp.maximum(m_old, tmax)
                alpha = jnp.exp2(m_old - m_new)
                p = jnp.exp2(s - m_new)
                l_ref[h] = alpha * l_ref[h] + jnp.sum(p, axis=0, keepdims=True)
                pv = lax.dot_general(v, p.astype(BF16), (((0,), (0,)), ((), ())),
                                     preferred_element_type=F32)
                acc_ref[h] = alpha * acc_ref[h] + pv
            m_ref[h] = m_new

    for i in range(DA_AHEAD):
        sbuf_ref[i] = scores(qi, i)

    def diag_body(j, carry):
        tile(j, True)
        return carry

    lax.fori_loop(qi, qi + 1, diag_body, 0)

    def body(i, carry):
        tile(qi - 1 - i, False)
        return carry

    lax.fori_loop(0, qi, body, 0)

    post = sub_ref[...] * (1.0 - lambda_init)
    for h in range(DA_HEADS):
        hs = slice(h * LANES, (h + 1) * LANES)
        o = acc_ref[h] * (1.0 / l_ref[h])
        attn = o[:, 0:tq] - lam * o[:, tq:2 * tq]
        y = attn * lax.rsqrt(jnp.mean(attn * attn, axis=0, keepdims=True) + EPS)
        g = g_ref[0, :, hs].astype(F32)
        o_ref[0, :, hs] = (y.T * post * _silu(g)).astype(BF16)


def _diff_attn(proj, lam_params, subln_w, lambda_init):
    b, s, _ = proj.shape
    tq = DA_TQ
    kern = functools.partial(_diff_attn_kernel, lambda_init=lambda_init)
    return pl.pallas_call(
        kern,
        grid=(b, s // tq),
        in_specs=[
            _const_spec((4, DA_HEAD_DIM)),
            _const_spec((1, 2 * DA_HEAD_DIM)),
            pl.BlockSpec((1, tq, DA_WIDTH), lambda bi, qi: (bi, qi, 0)),
            pl.BlockSpec((1, s, DA_WIDTH), lambda bi, qi: (bi, 0, 1)),
            pl.BlockSpec((1, s, DA_WIDTH), lambda bi, qi: (bi, 0, 2)),
            pl.BlockSpec((1, tq, DA_WIDTH), lambda bi, qi: (bi, qi, 3)),
        ],
        out_specs=pl.BlockSpec((1, tq, DA_WIDTH), lambda bi, qi: (bi, qi, 0)),
        out_shape=jax.ShapeDtypeStruct((b, s, DA_WIDTH), BF16),
        scratch_shapes=[
            pltpu.VMEM((DA_HEADS, 2 * LANES, 2 * tq), BF16),
            pltpu.VMEM((s, 2 * DA_WIDTH), BF16),
            pltpu.VMEM((tq, 2 * tq), F32),
            pltpu.VMEM((DA_HEADS, 1, 2 * tq), F32),
            pltpu.VMEM((DA_HEADS, 1, 2 * tq), F32),
            pltpu.VMEM((DA_HEADS, LANES, 2 * tq), F32),
            pltpu.VMEM((DA_AHEAD, tq, 2 * tq), F32),
        ],
        compiler_params=_params("arbitrary", "arbitrary"),
        name="diff_attn",
    )(lam_params, subln_w, proj, proj, proj, proj)


def _split3(x):
    hi = x.astype(BF16)
    r1 = x - hi.astype(F32)
    mid = r1.astype(BF16)
    lo = (r1 - mid.astype(F32)).astype(BF16)
    return hi, mid, lo


def _expand_heads(x, r_ref):
    r = r_ref[...]
    hi, mid, lo = _split3(x)
    return (jnp.dot(hi, r, preferred_element_type=F32) + jnp.dot(mid, r, preferred_element_type=F32)
            + jnp.dot(lo, r, preferred_element_type=F32))


def _conv_silu(raw, tail_ref, w_ref, b_ref):
    rows = raw.shape[0]
    ext = jnp.concatenate([tail_ref[...], raw], axis=0)
    tail_ref[...] = raw[rows - TAIL:rows, :]
    acc = b_ref[...] + w_ref[SSD_CONV - 1:SSD_CONV, :] * raw
    for d in range(1, SSD_CONV):
        shifted = pltpu.roll(ext, d, axis=0)[TAIL:TAIL + rows, :]
        acc = acc + w_ref[SSD_CONV - 1 - d:SSD_CONV - d, :] * shifted
    return _silu(acc)


def _ssd_kernel(z_ref, xs_ref, bc_ref, dt_ref, cwx_ref, cbx_ref, cwb_ref, cbb_ref, dtb_ref, alog_ref,
                dskip_ref, nw_ref, r_ref, o_ref,
                tailx_ref, tailb_ref, state_ref):
    L = SSD_CHUNK
    hpg = SSD_HEADS // SSD_GROUPS
    gw = SSD_WIDTH // SSD_GROUPS

    @pl.when(pl.program_id(1) == 0)
    def _():
        tailx_ref[...] = jnp.zeros_like(tailx_ref)
        tailb_ref[...] = jnp.zeros_like(tailb_ref)
        state_ref[...] = jnp.zeros_like(state_ref)

    xs = _conv_silu(xs_ref[0].astype(F32), tailx_ref, cwx_ref, cbx_ref)
    bc = _conv_silu(bc_ref[0].astype(F32), tailb_ref, cwb_ref, cbb_ref)
    bcb = bc.astype(BF16)

    draw = dt_ref[0] + dtb_ref[...]
    dt = jnp.maximum(draw, 0.0) + jnp.log1p(jnp.exp(-jnp.abs(draw)))
    a = -jnp.exp(alog_ref[...])
    da = dt * a

    row = lax.broadcasted_iota(jnp.int32, (L, L), 0)
    col = lax.broadcasted_iota(jnp.int32, (L, L), 1)
    tril = row >= col
    hi, mid, lo = _split3(da)
    ones = tril.astype(BF16)
    cs = (jnp.dot(ones, hi, preferred_element_type=F32) + jnp.dot(ones, mid, preferred_element_type=F32)
          + jnp.dot(ones, lo, preferred_element_type=F32))
    cs_t = cs.T
    cs_last = cs[L - 1:L, :]

    dt_e = _expand_heads(dt, r_ref)
    cs_e = _expand_heads(cs, r_ref)
    cs_last_e = cs_e[L - 1:L, :]
    xdt = xs * dt_e
    xdt_b = xdt.astype(BF16)
    w_state = (xdt * jnp.exp(cs_last_e - cs_e)).astype(BF16)
    e_in = jnp.exp(cs_e)
    chunk_decay = jnp.exp(cs_last_e)

    lane = lax.broadcasted_iota(jnp.int32, (L, LANES), 1)
    first_head = lane < SSD_HEAD_DIM
    zeros_b = jnp.zeros((L, LANES), BF16)

    y_parts = []
    for g in range(SSD_GROUPS):
        bg = bcb[:, g * SSD_STATE:(g + 1) * SSD_STATE]
        cg = bcb[:, (SSD_GROUPS + g) * SSD_STATE:(SSD_GROUPS + g + 1) * SSD_STATE]
        cb = lax.dot_general(cg, bg, (((1,), (1,)), ((), ())), preferred_element_type=F32)
        for jp in range(hpg // 2):
            pair = None
            for half in range(2):
                hh = g * hpg + 2 * jp + half
                seg = cs[:, hh:hh + 1] - cs_t[hh:hh + 1, :]
                lm = jnp.exp(jnp.where(tril, seg, NEG))
                mj = (cb * lm).astype(BF16)
                xp = xdt_b[:, (g * hpg + 2 * jp) * SSD_HEAD_DIM:(g * hpg + 2 * jp + 2) * SSD_HEAD_DIM]
                xp = jnp.where(first_head, xp, zeros_b) if half == 0 else jnp.where(first_head, zeros_b, xp)
                t = jnp.dot(mj, xp, preferred_element_type=F32)
                pair = t if pair is None else pair + t
            y_parts.append(pair)
        gs = slice(g * gw, (g + 1) * gw)
        prev = state_ref[g]
        y_off = jnp.dot(cg, prev.astype(BF16), preferred_element_type=F32) * e_in[:, gs]
        st = lax.dot_general(bg, w_state[:, gs], (((0,), (0,)), ((), ())), preferred_element_type=F32)
        state_ref[g] = prev * chunk_decay[:, gs] + st
        y_parts.append(y_off)

    nq = hpg // 2
    y_diag = jnp.concatenate(y_parts[0:nq] + y_parts[nq + 1:2 * nq + 1], axis=-1)
    y_off = jnp.concatenate([y_parts[nq], y_parts[2 * nq + 1]], axis=-1)
    y = y_diag + y_off + xs * dskip_ref[...]
    y = y * _silu(z_ref[0].astype(F32))
    outs = []
    for g in range(SSD_GROUPS):
        yg = y[:, g * gw:(g + 1) * gw]
        outs.append(yg * lax.rsqrt(jnp.mean(yg * yg, axis=-1, keepdims=True) + EPS))
    o_ref[0] = (jnp.concatenate(outs, axis=-1) * nw_ref[...]).astype(BF16)


def _ssd(proj, dt_raw, cw_x, cb_x, cw_b, cb_b, dt_bias, a_log, d_skip_e, norm_w, expand):
    b, s, _ = proj.shape
    L = SSD_CHUNK
    xs_blk = (4 * DA_WIDTH + SSD_WIDTH) // SSD_WIDTH
    z_blk = 4 * DA_WIDTH // SSD_WIDTH
    bc_blk = (4 * DA_WIDTH + 2 * SSD_WIDTH) // SSD_BC
    return pl.pallas_call(
        _ssd_kernel,
        grid=(b, s // L),
        in_specs=[
            pl.BlockSpec((1, L, SSD_WIDTH), lambda bi, c: (bi, c, z_blk)),
            pl.BlockSpec((1, L, SSD_WIDTH), lambda bi, c: (bi, c, xs_blk)),
            pl.BlockSpec((1, L, SSD_BC), lambda bi, c: (bi, c, bc_blk)),
            pl.BlockSpec((1, L, LANES), lambda bi, c: (bi, c, 0)),
            _const_spec((SSD_CONV, SSD_WIDTH)),
            _const_spec((1, SSD_WIDTH)),
            _const_spec((SSD_CONV, SSD_BC)),
            _const_spec((1, SSD_BC)),
            _const_spec((1, LANES)),
            _const_spec((1, LANES)),
            _const_spec((1, SSD_WIDTH)),
            _const_spec((1, SSD_WIDTH)),
            _const_spec((LANES, SSD_WIDTH)),
        ],
        out_specs=pl.BlockSpec((1, L, SSD_WIDTH), lambda bi, c: (bi, c, 0)),
        out_shape=jax.ShapeDtypeStruct((b, s, SSD_WIDTH), BF16),
        scratch_shapes=[
            pltpu.VMEM((TAIL, SSD_WIDTH), F32),
            pltpu.VMEM((TAIL, SSD_BC), F32),
            pltpu.VMEM((SSD_GROUPS, SSD_STATE, SSD_WIDTH // SSD_GROUPS), F32),
        ],
        compiler_params=_params("arbitrary", "arbitrary"),
        name="ssd",
    )(proj, proj, proj, dt_raw, cw_x, cb_x, cw_b, cb_b, dt_bias, a_log, d_skip_e, norm_w, expand)


MID_TM = 512


def _mid_proj_kernel(x_ref, ya_ref, yb_ref, wa_ref, wb_ref, nw_ref, win_ref, bin_ref, x1_ref, p_ref):
    x1 = (x_ref[...] + jnp.dot(ya_ref[...], wa_ref[...], preferred_element_type=F32)
          + jnp.dot(yb_ref[...], wb_ref[...], preferred_element_type=F32))
    x1_ref[...] = x1
    ms = jnp.mean(x1 * x1, axis=-1, keepdims=True)
    hn = (x1 * lax.rsqrt(ms + EPS) * nw_ref[...]).astype(BF16)
    nq = SW_WIDTH
    p_ref[:, 0:nq] = ((jnp.dot(hn, win_ref[:, 0:nq], preferred_element_type=F32) + bin_ref[:, 0:nq])
                      * SW_QSCALE).astype(BF16)
    p_ref[:, nq:] = (jnp.dot(hn, win_ref[:, nq:], preferred_element_type=F32) + bin_ref[:, nq:]).astype(BF16)


def _mid_proj(x2, ya, yb, w_a, w_b, norm_w, w_in, b_in):
    m = x2.shape[0]
    n = w_in.shape[1]
    tm = MID_TM
    row = lambda i: (i, 0)
    return pl.pallas_call(
        _mid_proj_kernel,
        grid=(m // tm,),
        in_specs=[
            pl.BlockSpec((tm, D_MODEL), row),
            pl.BlockSpec((tm, DA_WIDTH), row),
            pl.BlockSpec((tm, SSD_WIDTH), row),
            _const_spec((DA_WIDTH, D_MODEL)),
            _const_spec((SSD_WIDTH, D_MODEL)),
            _const_spec((1, D_MODEL)),
            _const_spec((D_MODEL, n)),
            _const_spec((1, n)),
        ],
        out_specs=[pl.BlockSpec((tm, D_MODEL), row), pl.BlockSpec((tm, n), row)],
        out_shape=[jax.ShapeDtypeStruct((m, D_MODEL), F32), jax.ShapeDtypeStruct((m, n), BF16)],
        compiler_params=_params("arbitrary"),
        name="mid_proj",
    )(x2, ya, yb, w_a, w_b, norm_w, w_in, b_in)


def _swa_out_kernel(sink_ref, q_ref, g_ref, kvp_ref, kvc_ref, x1_ref, wo_ref, fw_ref, o_ref,
                    qq_ref, ka_ref, mask_ref, gated_ref):
    W = WINDOW
    nblk = pl.program_id(1)
    hpg = SW_GQ

    lane = lax.broadcasted_iota(jnp.int32, (W, LANES), 1)
    even = (lane & 1) == 0

    @pl.when(nblk == 0)
    def _():
        krel = lax.broadcasted_iota(jnp.int32, (2 * W, W), 0)
        qrel = lax.broadcasted_iota(jnp.int32, (2 * W, W), 1) + W
        dist = qrel - krel
        band = (dist >= 0) & (dist < W)
        mask_ref[0] = jnp.where(band, 0.0, NEG)
        mask_ref[1] = jnp.where(band & (krel >= W), 0.0, NEG)
        lane2 = lax.broadcasted_iota(jnp.int32, (2 * W, LANES), 1)
        krow = lax.broadcasted_iota(jnp.int32, (2 * W, LANES), 0).astype(F32)
        kext = jnp.where(lane2 < 2 * N_SPLIT, jnp.where((lane2 & 1) == 0, krow, 1.0), 0.0).astype(BF16)
        qrow = lax.broadcasted_iota(jnp.int32, (W, LANES), 0).astype(F32) + float(W)
        for g in range(SW_KV_HEADS):
            ka_ref[g, :, LANES:2 * LANES] = kext
            for hl in range(hpg):
                slope2 = _alibi_slope(g * hpg + hl, SW_HEADS) * LOG2E
                qq_ref[g, LANES:2 * LANES, hl * W:(hl + 1) * W] = _piece_select(
                    jnp.where(even, slope2, -slope2 * qrow), lane).astype(F32).T.astype(BF16)

    kv = jnp.concatenate([kvp_ref[0], kvc_ref[0]], axis=0)
    k2 = kv[:, 0:LANES].astype(F32)
    vblk = kv[:, LANES:2 * LANES]
    k2r = pltpu.roll(k2, SW_HEAD_DIM, axis=1)
    low2 = lax.broadcasted_iota(jnp.int32, (2 * W, LANES), 1) < SW_HEAD_DIM
    ka_ref[0, :, 0:LANES] = jnp.where(low2, k2, k2r).astype(BF16)
    ka_ref[1, :, 0:LANES] = jnp.where(low2, k2r, k2).astype(BF16)

    low_t = lax.broadcasted_iota(jnp.int32, (LANES, W), 0) < SW_HEAD_DIM
    for hp in range(SW_HEADS // 2):
        g, hl = (2 * hp) // hpg, (2 * hp) % hpg
        qt = q_ref[0, :, hp * LANES:(hp + 1) * LANES].astype(F32).T
        qq_ref[g, 0:LANES, hl * W:(hl + 1) * W] = jnp.where(low_t, qt, 0.0).astype(BF16)
        qq_ref[g, 0:LANES, (hl + 1) * W:(hl + 2) * W] = jnp.where(low_t, 0.0, qt).astype(BF16)

    mask = mask_ref[jnp.where(nblk == 0, 1, 0)]
    scores = [jnp.dot(ka_ref[g], qq_ref[g], preferred_element_type=F32)
              for g in range(SW_KV_HEADS)]
    for g in range(SW_KV_HEADS):
        probs, inv = [], []
        for hl in range(hpg):
            h = g * hpg + hl
            s = mask + scores[g][:, hl * W:(hl + 1) * W]
            sink2 = sink_ref[0:1, h:h + 1] * LOG2E
            m = jnp.maximum(jnp.max(s, axis=0, keepdims=True), sink2)
            p = jnp.exp2(s - m)
            inv.append(1.0 / (jnp.sum(p, axis=0, keepdims=True) + jnp.exp2(sink2 - m)))
            probs.append(p.astype(BF16))
        pt = jnp.concatenate(probs, axis=1)
        ot = lax.dot_general(vblk, pt, (((0,), (0,)), ((), ())), preferred_element_type=F32)
        ot = ot[g * SW_HEAD_DIM:(g + 1) * SW_HEAD_DIM, :] * jnp.concatenate(inv, axis=1)
        for i in range(hpg // 2):
            pair = jnp.concatenate([ot[:, (2 * i) * W:(2 * i + 1) * W],
                                    ot[:, (2 * i + 1) * W:(2 * i + 2) * W]], axis=0)
            hs = slice((g * hpg // 2 + i) * LANES, (g * hpg // 2 + i + 1) * LANES)
            gate = g_ref[0, :, hs].astype(F32)
            gated_ref[:, hs] = (pair.T * _silu(gate)).astype(BF16)

    x2 = x1_ref[0] + jnp.dot(gated_ref[...], wo_ref[...], preferred_element_type=F32)
    ms = jnp.mean(x2 * x2, axis=-1, keepdims=True)
    o_ref[0] = x2 * lax.rsqrt(ms + EPS) * fw_ref[...]


def _swa_out(proj_c, x1, sinks, w_out, final_w):
    b, s, _ = proj_c.shape
    W = WINDOW
    kv_blk = 2 * SW_WIDTH // (2 * LANES)
    return pl.pallas_call(
        _swa_out_kernel,
        grid=(b, s // W),
        in_specs=[
            _const_spec((1, LANES)),
            pl.BlockSpec((1, W, SW_WIDTH), lambda bi, n: (bi, n, 0)),
            pl.BlockSpec((1, W, SW_WIDTH), lambda bi, n: (bi, n, 1)),
            pl.BlockSpec((1, W, 2 * LANES), lambda bi, n: (bi, jnp.maximum(n - 1, 0), kv_blk)),
            pl.BlockSpec((1, W, 2 * LANES), lambda bi, n: (bi, n, kv_blk)),
            pl.BlockSpec((1, W, D_MODEL), lambda bi, n: (bi, n, 0)),
            _const_spec((SW_WIDTH, D_MODEL)),
            _const_spec((1, D_MODEL)),
        ],
        out_specs=pl.BlockSpec((1, W, D_MODEL), lambda bi, n: (bi, n, 0)),
        out_shape=jax.ShapeDtypeStruct((b, s, D_MODEL), F32),
        scratch_shapes=[
            pltpu.VMEM((SW_KV_HEADS, 2 * LANES, SW_GQ * W), BF16),
            pltpu.VMEM((SW_KV_HEADS, 2 * W, 2 * LANES), BF16),
            pltpu.VMEM((2, 2 * W, W), F32),
            pltpu.VMEM((W, SW_WIDTH), BF16),
        ],
        compiler_params=_params("arbitrary", "arbitrary"),
        name="swa_out",
    )(sinks, proj_c, proj_c, proj_c, proj_c, x1, w_out, final_w)


def _even_layer(x, norm_w, w_in, conv_w, conv_b, dt_bias, a_log, d_skip, ssd_norm_w,
                lq1, lk1, lq2, lk2, subln_w, lambda_init):
    b, s, d = x.shape
    w_main = w_in[:, :EVEN_MAIN].astype(BF16)
    w_dt = jnp.pad(w_in[:, EVEN_MAIN:], ((0, 0), (0, LANES - SSD_HEADS))).astype(BF16)
    proj, dt_raw = _inproj_even(x.reshape(b * s, d), norm_w.reshape(1, d), w_main, w_dt)
    proj = proj.reshape(b, s, EVEN_MAIN)
    dt_raw = dt_raw.reshape(b, s, LANES)

    lam_params = jnp.stack([lq1, lk1, lq2, lk2]).astype(F32)
    y_a = _diff_attn(proj, lam_params, subln_w.reshape(1, -1).astype(F32), lambda_init)

    pad = lambda v: jnp.pad(v.astype(F32), (0, LANES - SSD_HEADS)).reshape(1, LANES)
    expand = (jnp.arange(LANES)[:, None] == (jnp.arange(SSD_WIDTH)[None, :] // SSD_HEAD_DIM)).astype(BF16)
    y_b = _ssd(proj, dt_raw,
               conv_w[:, :SSD_WIDTH], conv_b[:SSD_WIDTH].reshape(1, -1),
               conv_w[:, SSD_WIDTH:], conv_b[SSD_WIDTH:].reshape(1, -1),
               pad(dt_bias), pad(a_log),
               jnp.repeat(d_skip.astype(F32), SSD_HEAD_DIM).reshape(1, -1),
               ssd_norm_w.reshape(1, -1).astype(F32), expand)
    return y_a, y_b


def kernel(x, norm_a, w_in_a, conv_w_a, conv_b_a, dt_bias_a, a_log_a, d_skip_a, ssd_norm_a,
           lambda_q1_a, lambda_k1_a, lambda_q2_a, lambda_k2_a, subln_a, w_out_a,
           norm_c, w_in_c, b_in_c, sinks_c, w_out_c, final_norm):
    b, s, d = x.shape
    lambda_init = 0.8 - 0.6 * math.exp(-0.3 * 0)
    y_a, y_b = _even_layer(x, norm_a[0], w_in_a[0], conv_w_a[0], conv_b_a[0], dt_bias_a[0], a_log_a[0],
                           d_skip_a[0], ssd_norm_a[0], lambda_q1_a[0], lambda_k1_a[0], lambda_q2_a[0],
                           lambda_k2_a[0], subln_a[0], lambda_init)

    kv0 = SW_WIDTH
    g0 = SW_WIDTH + 2 * SW_KV_HEADS * SW_HEAD_DIM
    perm = lambda w: jnp.concatenate([w[..., :kv0], w[..., g0:], w[..., kv0:g0]], axis=-1)
    w_in = perm(w_in_c[0]).astype(BF16)
    b_in = perm(b_in_c[0]).reshape(1, -1).astype(F32)
    w_out = w_out_a[0].astype(BF16)
    x1, proj_c = _mid_proj(x.reshape(b * s, d), y_a.reshape(b * s, -1), y_b.reshape(b * s, -1),
                           w_out[:DA_WIDTH], w_out[DA_WIDTH:], norm_c[0].reshape(1, d), w_in, b_in)

    sinks = jnp.pad(sinks_c[0].astype(F32), (0, LANES - SW_HEADS)).reshape(1, LANES)
    return _swa_out(proj_c.reshape(b, s, -1), x1.reshape(b, s, d), sinks,
                    w_out_c[0].astype(BF16), final_norm.reshape(1, d))
```

```python
import functools
import math

import jax
import jax.numpy as jnp
from jax import lax
from jax.experimental import pallas as pl
from jax.experimental.pallas import tpu as pltpu

F32 = jnp.float32
BF16 = jnp.bfloat16

D_MODEL = 1024
EPS = 1e-5

DA_HEADS = 8
DA_HEAD_DIM = 64
DA_WIDTH = DA_HEADS * 2 * DA_HEAD_DIM

SSD_WIDTH = 1024
SSD_HEAD_DIM = 64
SSD_HEADS = SSD_WIDTH // SSD_HEAD_DIM
SSD_GROUPS = 2
SSD_STATE = 128
SSD_CONV = 4
SSD_CHUNK = 128
SSD_BC = 2 * SSD_GROUPS * SSD_STATE

SW_HEADS = 16
SW_KV_HEADS = 2
SW_GQ = SW_HEADS // SW_KV_HEADS
SW_HEAD_DIM = 64
SW_WIDTH = SW_HEADS * SW_HEAD_DIM
WINDOW = 128

EVEN_MAIN = 4 * DA_WIDTH + SSD_WIDTH + SSD_WIDTH + SSD_BC
LANES = 128
TAIL = 8
NEG = -1e30
LOG2E = math.log2(math.e)
DA_QSCALE = DA_HEAD_DIM ** -0.5 * LOG2E
SW_QSCALE = SW_HEAD_DIM ** -0.5 * LOG2E

VMEM_LIMIT = 56 * 1024 * 1024


def _alibi_slope(i, n):
    return float(2.0 ** (-8.0 * (i + 1) / n))


def _sigmoid(x):
    return 1.0 / (1.0 + jnp.exp(-x))


def _silu(x):
    return x * _sigmoid(x)


def _params(*sem):
    return pltpu.CompilerParams(dimension_semantics=sem, vmem_limit_bytes=VMEM_LIMIT)


def _const_spec(shape):
    nd = len(shape)
    return pl.BlockSpec(shape, lambda *_: (0,) * nd, pipeline_mode=pl.Buffered(1))


DA_TQ = 256
DA_AHEAD = 2
N_SPLIT = 3


def _piece_select(x, lane):
    hi, mid, lo = _split3(x)
    zero = jnp.zeros_like(hi)
    return jnp.where(lane < 2, hi, jnp.where(lane < 4, mid, jnp.where(lane < 2 * N_SPLIT, lo, zero)))


def _diff_attn_kernel(lam_ref, sub_ref, q_ref, k_ref, v_ref, g_ref, o_ref,
                      qa_ref, kaug_ref, dmask_ref, m_ref, l_ref, acc_ref, sbuf_ref, *, lambda_init):
    tq = DA_TQ
    tk = DA_TQ
    qi = pl.program_id(1)

    lp = lam_ref[...]
    lam = (jnp.exp(jnp.sum(lp[0:1] * lp[1:2], axis=-1, keepdims=True))
           - jnp.exp(jnp.sum(lp[2:3] * lp[3:4], axis=-1, keepdims=True)) + lambda_init)

    lane = lax.broadcasted_iota(jnp.int32, (tq, LANES), 1)
    rowf = lax.broadcasted_iota(jnp.int32, (tq, LANES), 0).astype(F32)
    even = (lane & 1) == 0

    @pl.when(qi == 0)
    def _():
        krow = lax.broadcasted_iota(jnp.int32, (tk, 2 * tq), 0)
        qcol = lax.broadcasted_iota(jnp.int32, (tk, 2 * tq), 1)
        qcol = jnp.where(qcol >= tq, qcol - tq, qcol)
        dmask_ref[...] = jnp.where(krow <= qcol, 0.0, NEG)
        kext = jnp.where(lane < 2 * N_SPLIT, jnp.where(even, rowf, 1.0), 0.0).astype(BF16)
        qext_t = _piece_select(jnp.where(even, LOG2E, -LOG2E * rowf), lane).astype(F32).T
        for h in range(DA_HEADS):
            qext_h = (qext_t * _alibi_slope(h, DA_HEADS)).astype(BF16)
            qa_ref[h, LANES:2 * LANES, 0:tq] = qext_h
            qa_ref[h, LANES:2 * LANES, tq:2 * tq] = qext_h
        for t in range(k_ref.shape[1] // tk):
            rows = slice(t * tk, (t + 1) * tk)
            for h in range(DA_HEADS):
                kaug_ref[rows, 2 * h * LANES:(2 * h + 1) * LANES] = k_ref[0, rows, h * LANES:(h + 1) * LANES]
                kaug_ref[rows, (2 * h + 1) * LANES:(2 * h + 2) * LANES] = kext

    first_map = lax.broadcasted_iota(jnp.int32, (LANES, tq), 0) < DA_HEAD_DIM
    for h in range(DA_HEADS):
        qt = q_ref[0, :, h * LANES:(h + 1) * LANES].astype(F32).T
        qa_ref[h, 0:LANES, 0:tq] = jnp.where(first_map, qt, 0.0).astype(BF16)
        qa_ref[h, 0:LANES, tq:2 * tq] = jnp.where(first_map, 0.0, qt).astype(BF16)

    def scores(j, h):
        kst = pl.multiple_of(j * tk, tk)
        ka = kaug_ref[pl.ds(kst, tk), 2 * h * LANES:(2 * h + 2) * LANES]
        return jnp.dot(ka, qa_ref[h], preferred_element_type=F32)

    def tile(j, diagonal):
        kst = pl.multiple_of(j * tk, tk)
        j_next = jnp.maximum(j - 1, 0)
        pending = [sbuf_ref[i] for i in range(DA_AHEAD)]
        for h in range(DA_HEADS):
            hs = slice(h * LANES, (h + 1) * LANES)
            s = pending.pop(0)
            if h + DA_AHEAD < DA_HEADS:
                pending.append(scores(j, h + DA_AHEAD))
            else:
                sbuf_ref[h + DA_AHEAD - DA_HEADS] = scores(j_next, h + DA_AHEAD - DA_HEADS)
            v = v_ref[0, pl.ds(kst, tk), hs]
            if diagonal:
                s = dmask_ref[...] + s
            tmax = jnp.max(s, axis=0, keepdims=True)
            if diagonal:
                m_new = tmax
                p = jnp.exp2(s - m_new)
                l_ref[h] = jnp.sum(p, axis=0, keepdims=True)
                pv = lax.dot_general(v, p.astype(BF16), (((0,), (0,)), ((), ())),
                                     preferred_element_type=F32)
                acc_ref[h] = pv
            else:
                m_old = m_ref[h] + _alibi_slope(h, DA_HEADS) * LOG2E * tk
                m_new = jnp.maximum(m_old, tmax)
                alpha = jnp.exp2(m_old - m_new)
                p = jnp.exp2(s - m_new)
                l_ref[h] = alpha * l_ref[h] + jnp.sum(p, axis=0, keepdims=True)
                pv = lax.dot_general(v, p.astype(BF16), (((0,), (0,)), ((), ())),
                                     preferred_element_type=F32)
                acc_ref[h] = alpha * acc_ref[h] + pv
            m_ref[h] = m_new

    for i in range(DA_AHEAD):
        sbuf_ref[i] = scores(qi, i)

    def diag_body(j, carry):
        tile(j, True)
        return carry

    lax.fori_loop(qi, qi + 1, diag_body, 0)

    def body(i, carry):
        tile(qi - 1 - i, False)
        return carry

    lax.fori_loop(0, qi, body, 0)

    post = sub_ref[...] * (1.0 - lambda_init)
    for h in range(DA_HEADS):
        hs = slice(h * LANES, (h + 1) * LANES)
        o = acc_ref[h] * (1.0 / l_ref[h])
        attn = o[:, 0:tq] - lam * o[:, tq:2 * tq]
        y = attn * lax.rsqrt(jnp.mean(attn * attn, axis=0, keepdims=True) + EPS)
        g = g_ref[0, :, hs].astype(F32)
        o_ref[0, :, hs] = (y.T * post * _silu(g)).astype(BF16)


def _diff_attn(proj, lam_params, subln_w, lambda_init):
    b, s, _ = proj.shape
    tq = DA_TQ
    kern = functools.partial(_diff_attn_kernel, lambda_init=lambda_init)
    return pl.pallas_call(
        kern,
        grid=(b, s // tq),
        in_specs=[
            _const_spec((4, DA_HEAD_DIM)),
            _const_spec((1, 2 * DA_HEAD_DIM)),
            pl.BlockSpec((1, tq, DA_WIDTH), lambda bi, qi: (bi, qi, 0)),
            pl.BlockSpec((1, s, DA_WIDTH), lambda bi, qi: (bi, 0, 1)),
            pl.BlockSpec((1, s, DA_WIDTH), lambda bi, qi: (bi, 0, 2)),
            pl.BlockSpec((1, tq, DA_WIDTH), lambda bi, qi: (bi, qi, 3)),
        ],
        out_specs=pl.BlockSpec((1, tq, DA_WIDTH), lambda bi, qi: (bi, qi, 0)),
        out_shape=jax.ShapeDtypeStruct((b, s, DA_WIDTH), BF16),
        scratch_shapes=[
            pltpu.VMEM((DA_HEADS, 2 * LANES, 2 * tq), BF16),
            pltpu.VMEM((s, 2 * DA_WIDTH), BF16),
            pltpu.VMEM((tq, 2 * tq), F32),
            pltpu.VMEM((DA_HEADS, 1, 2 * tq), F32),
            pltpu.VMEM((DA_HEADS, 1, 2 * tq), F32),
            pltpu.VMEM((DA_HEADS, LANES, 2 * tq), F32),
            pltpu.VMEM((DA_AHEAD, tq, 2 * tq), F32),
        ],
        compiler_params=_params("arbitrary", "arbitrary"),
        name="diff_attn",
    )(lam_params, subln_w, proj, proj, proj, proj)


def _split3(x):
    hi = x.astype(BF16)
    r1 = x - hi.astype(F32)
    mid = r1.astype(BF16)
    lo = (r1 - mid.astype(F32)).astype(BF16)
    return hi, mid, lo


def _expand_heads(x, r_ref):
    r = r_ref[...]
    hi, mid, lo = _split3(x)
    return (jnp.dot(hi, r, preferred_element_type=F32) + jnp.dot(mid, r, preferred_element_type=F32)
            + jnp.dot(lo, r, preferred_element_type=F32))


def _conv_silu(raw, tail_ref, w_ref, b_ref, keep):
    rows = raw.shape[0]
    tail = tail_ref[...]
    if keep is not None:
        tail = jnp.where(keep, tail, 0.0)
    ext = jnp.concatenate([tail, raw], axis=0)
    tail_ref[...] = raw[rows - TAIL:rows, :]
    acc = b_ref[...] + w_ref[SSD_CONV - 1:SSD_CONV, :] * raw
    for d in range(1, SSD_CONV):
        shifted = pltpu.roll(ext, d, axis=0)[TAIL:TAIL + rows, :]
        acc = acc + w_ref[SSD_CONV - 1 - d:SSD_CONV - d, :] * shifted
    return _silu(acc)


def _ssd_chunk(z_raw, xs_raw, bc_raw, dt_raw, keep, params, tailx_ref, tailb_ref, state_ref):
    cwx_ref, cbx_ref, cwb_ref, cbb_ref, dtb_ref, alog_ref, dskip_ref, nw_ref, r_ref = params
    L = SSD_CHUNK
    hpg = SSD_HEADS // SSD_GROUPS
    gw = SSD_WIDTH // SSD_GROUPS
    nt = (((1,), (1,)), ((), ()))
    tn = (((0,), (0,)), ((), ()))

    draw = dt_raw + dtb_ref[...]
    dt = jnp.maximum(draw, 0.0) + jnp.log1p(jnp.exp(-jnp.abs(draw)))
    a = -jnp.exp(alog_ref[...])
    da = dt * a
    row = lax.broadcasted_iota(jnp.int32, (L, L), 0)
    col = lax.broadcasted_iota(jnp.int32, (L, L), 1)
    tril = row >= col
    hi, mid, lo = _split3(da)
    ones = tril.astype(BF16)
    yield None
    cs = (jnp.dot(ones, hi, preferred_element_type=F32) + jnp.dot(ones, mid, preferred_element_type=F32)
          + jnp.dot(ones, lo, preferred_element_type=F32))
    dt_e = _expand_heads(dt, r_ref)

    xs = _conv_silu(xs_raw.astype(F32), tailx_ref, cwx_ref, cbx_ref, keep)
    bc = _conv_silu(bc_raw.astype(F32), tailb_ref, cwb_ref, cbb_ref, keep)
    bcb = bc.astype(BF16)
    cs_t = cs.T
    yield None
    cs_e = _expand_heads(cs, r_ref)
    bgs = [bcb[:, g * SSD_STATE:(g + 1) * SSD_STATE] for g in range(SSD_GROUPS)]
    cgs = [bcb[:, (SSD_GROUPS + g) * SSD_STATE:(SSD_GROUPS + g + 1) * SSD_STATE] for g in range(SSD_GROUPS)]
    cbs = [lax.dot_general(cgs[g], bgs[g], nt, preferred_element_type=F32) for g in range(SSD_GROUPS)]

    cs_last_e = cs_e[L - 1:L, :]
    xdt = xs * dt_e
    xdt_b = xdt.astype(BF16)
    w_state = (xdt * jnp.exp(cs_last_e - cs_e)).astype(BF16)
    e_in = jnp.exp(cs_e)
    chunk_decay = jnp.exp(cs_last_e)

    lane = lax.broadcasted_iota(jnp.int32, (L, LANES), 1)
    first_head = lane < SSD_HEAD_DIM
    zeros_b = jnp.zeros((L, LANES), BF16)

    y_diag, y_off = [], []
    for g in range(SSD_GROUPS):
        for jp in range(hpg // 2):
            pair = None
            for half in range(2):
                hh = g * hpg + 2 * jp + half
                seg = cs[:, hh:hh + 1] - cs_t[hh:hh + 1, :]
                lm = jnp.exp(jnp.where(tril, seg, NEG))
                mj = (cbs[g] * lm).astype(BF16)
                xp = xdt_b[:, (g * hpg + 2 * jp) * SSD_HEAD_DIM:(g * hpg + 2 * jp + 2) * SSD_HEAD_DIM]
                xp = jnp.where(first_head, xp, zeros_b) if half == 0 else jnp.where(first_head, zeros_b, xp)
                yield None
                t = jnp.dot(mj, xp, preferred_element_type=F32)
                pair = t if pair is None else pair + t
            y_diag.append(pair)
        gs = slice(g * gw, (g + 1) * gw)
        prev = state_ref[g]
        if keep is not None:
            prev = jnp.where(keep, prev, 0.0)
        yield None
        y_off.append(jnp.dot(cgs[g], prev.astype(BF16), preferred_element_type=F32) * e_in[:, gs])
        st = lax.dot_general(bgs[g], w_state[:, gs], tn, preferred_element_type=F32)
        state_ref[g] = prev * chunk_decay[:, gs] + st

    y = jnp.concatenate(y_diag, axis=-1) + jnp.concatenate(y_off, axis=-1) + xs * dskip_ref[...]
    y = y * _silu(z_raw.astype(F32))
    outs = []
    for g in range(SSD_GROUPS):
        yg = y[:, g * gw:(g + 1) * gw]
        outs.append(yg * lax.rsqrt(jnp.mean(yg * yg, axis=-1, keepdims=True) + EPS))
    yield (jnp.concatenate(outs, axis=-1) * nw_ref[...]).astype(BF16)


EVEN_RB = 512
EVEN_TN = 256
SSD_GAPS = 2 + SSD_HEADS + SSD_GROUPS
ATTN_COLS = 4 * DA_WIDTH
SSD_COLS = EVEN_MAIN - ATTN_COLS


def _even_front_kernel(x_ref, nw_ref, w_ref, wdt_ref, cwx_ref, cbx_ref, cwb_ref, cbb_ref, dtb_ref, alog_ref,
                       dskip_ref, snw_ref, r_ref, o_ref, y_ref,
                       hn_ref, sbuf_ref, dbuf_ref, tailx_ref, tailb_ref, state_ref, *, seq_blocks):
    i = pl.program_id(0)
    L = SSD_CHUNK

    @pl.when(i == 0)
    def _():
        sbuf_ref[...] = jnp.zeros_like(sbuf_ref)
        dbuf_ref[...] = jnp.zeros_like(dbuf_ref)
        tailx_ref[...] = jnp.zeros_like(tailx_ref)
        tailb_ref[...] = jnp.zeros_like(tailb_ref)
        state_ref[...] = jnp.zeros_like(state_ref)

    wr = lax.rem(i, 2)
    rd = 1 - wr
    starts_sequence = lax.rem(i + seq_blocks - 1, seq_blocks) == 0

    x = x_ref[...]
    ms = jnp.mean(x * x, axis=-1, keepdims=True)
    hn_ref[...] = (x * lax.rsqrt(ms + EPS) * nw_ref[...]).astype(BF16)

    def project(j):
        sl = slice(j * EVEN_TN, (j + 1) * EVEN_TN)
        acc = jnp.dot(hn_ref[...], w_ref[:, sl], preferred_element_type=F32)
        if (j + 1) * EVEN_TN <= DA_WIDTH:
            acc = acc * DA_QSCALE
        if (j + 1) * EVEN_TN <= ATTN_COLS:
            o_ref[:, sl] = acc.astype(BF16)
        else:
            sbuf_ref[wr, :, j * EVEN_TN - ATTN_COLS:(j + 1) * EVEN_TN - ATTN_COLS] = acc.astype(BF16)

    n_proj = EVEN_MAIN // EVEN_TN
    n_chunks = EVEN_RB // L
    gaps = n_chunks * SSD_GAPS
    params = (cwx_ref, cbx_ref, cwb_ref, cbb_ref, dtb_ref, alog_ref, dskip_ref, snw_ref, r_ref)
    done = 0
    gap = 0
    for c in range(n_chunks):
        rows = slice(c * L, (c + 1) * L)
        keep = jnp.logical_not(starts_sequence) if c == 0 else None
        for out in _ssd_chunk(sbuf_ref[rd, rows, 0:SSD_WIDTH], sbuf_ref[rd, rows, SSD_WIDTH:2 * SSD_WIDTH],
                              sbuf_ref[rd, rows, 2 * SSD_WIDTH:SSD_COLS], dbuf_ref[rd, rows, :],
                              keep, params, tailx_ref, tailb_ref, state_ref):
            if out is None:
                gap += 1
                while done < n_proj and done * gaps < gap * n_proj:
                    project(done)
                    done += 1
            else:
                y_ref[rows, :] = out
    while done < n_proj:
        project(done)
        done += 1
    dbuf_ref[wr] = jnp.dot(hn_ref[...], wdt_ref[...], preferred_element_type=F32)


def _even_front(x2, norm_w, w_main, w_dt, cw_x, cb_x, cw_b, cb_b, dt_bias, a_log, d_skip_e, ssd_norm_w, expand,
                seq_len):
    m = x2.shape[0]
    rb = EVEN_RB
    nb = m // rb
    kern = functools.partial(_even_front_kernel, seq_blocks=seq_len // rb)
    return pl.pallas_call(
        kern,
        grid=(nb + 1,),
        in_specs=[
            pl.BlockSpec((rb, D_MODEL), lambda i: (jnp.minimum(i, nb - 1), 0)),
            _const_spec((1, D_MODEL)),
            _const_spec((D_MODEL, EVEN_MAIN)),
            _const_spec((D_MODEL, LANES)),
            _const_spec((SSD_CONV, SSD_WIDTH)),
            _const_spec((1, SSD_WIDTH)),
            _const_spec((SSD_CONV, SSD_BC)),
            _const_spec((1, SSD_BC)),
            _const_spec((1, LANES)),
            _const_spec((1, LANES)),
            _const_spec((1, SSD_WIDTH)),
            _const_spec((1, SSD_WIDTH)),
            _const_spec((LANES, SSD_WIDTH)),
        ],
        out_specs=[
            pl.BlockSpec((rb, ATTN_COLS), lambda i: (jnp.minimum(i, nb - 1), 0)),
            pl.BlockSpec((rb, SSD_WIDTH), lambda i: (jnp.maximum(i - 1, 0), 0)),
        ],
        out_shape=[
            jax.ShapeDtypeStruct((m, ATTN_COLS), BF16),
            jax.ShapeDtypeStruct((m, SSD_WIDTH), BF16),
        ],
        scratch_shapes=[
            pltpu.VMEM((rb, D_MODEL), BF16),
            pltpu.VMEM((2, rb, SSD_COLS), BF16),
            pltpu.VMEM((2, rb, LANES), F32),
            pltpu.VMEM((TAIL, SSD_WIDTH), F32),
            pltpu.VMEM((TAIL, SSD_BC), F32),
            pltpu.VMEM((SSD_GROUPS, SSD_STATE, SSD_WIDTH // SSD_GROUPS), F32),
        ],
        compiler_params=_params("arbitrary"),
        name="even_front",
    )(x2, norm_w, w_main, w_dt, cw_x, cb_x, cw_b, cb_b, dt_bias, a_log, d_skip_e, ssd_norm_w, expand)


BACK_TN = 256
ODD_IN = 2 * SW_WIDTH + 2 * LANES


def _back_kernel(x_ref, ya_ref, yb_ref, wa_ref, wb_ref, nw_ref, win_ref, bin_ref, sink_ref, wo_ref, fw_ref,
                 o_ref, x1_buf, qg_buf, kv_buf, qq_ref, ka_ref, mask_ref, gated_ref, *, seq_blocks):
    W = WINDOW
    i = pl.program_id(0)
    hpg = SW_GQ

    lane = lax.broadcasted_iota(jnp.int32, (W, LANES), 1)
    even = (lane & 1) == 0

    @pl.when(i == 0)
    def _():
        x1_buf[...] = jnp.zeros_like(x1_buf)
        qg_buf[...] = jnp.zeros_like(qg_buf)
        kv_buf[...] = jnp.zeros_like(kv_buf)
        krel = lax.broadcasted_iota(jnp.int32, (2 * W, W), 0)
        qrel = lax.broadcasted_iota(jnp.int32, (2 * W, W), 1) + W
        dist = qrel - krel
        band = (dist >= 0) & (dist < W)
        mask_ref[0] = jnp.where(band, 0.0, NEG)
        mask_ref[1] = jnp.where(band & (krel >= W), 0.0, NEG)
        lane2 = lax.broadcasted_iota(jnp.int32, (2 * W, LANES), 1)
        krow = lax.broadcasted_iota(jnp.int32, (2 * W, LANES), 0).astype(F32)
        kext = jnp.where(lane2 < 2 * N_SPLIT, jnp.where((lane2 & 1) == 0, krow, 1.0), 0.0).astype(BF16)
        qrow = lax.broadcasted_iota(jnp.int32, (W, LANES), 0).astype(F32) + float(W)
        for g in range(SW_KV_HEADS):
            ka_ref[g, :, LANES:2 * LANES] = kext
            for hl in range(hpg):
                slope2 = _alibi_slope(g * hpg + hl, SW_HEADS) * LOG2E
                qq_ref[g, LANES:2 * LANES, hl * W:(hl + 1) * W] = _piece_select(
                    jnp.where(even, slope2, -slope2 * qrow), lane).astype(F32).T.astype(BF16)

    wr2 = lax.rem(i, 2)
    rd2 = 1 - wr2
    wr3 = lax.rem(i, 3)
    own3 = lax.rem(i + 2, 3)
    prev3 = lax.rem(i + 1, 3)
    first_window = lax.rem(i + seq_blocks - 1, seq_blocks) == 0

    proj_a = jnp.dot(ya_ref[...], wa_ref[...], preferred_element_type=F32)
    proj_b = jnp.dot(yb_ref[...], wb_ref[...], preferred_element_type=F32)

    kv = jnp.concatenate([kv_buf[prev3], kv_buf[own3]], axis=0)
    k2 = kv[:, 0:LANES].astype(F32)
    vblk = kv[:, LANES:2 * LANES]
    k2r = pltpu.roll(k2, SW_HEAD_DIM, axis=1)
    low2 = lax.broadcasted_iota(jnp.int32, (2 * W, LANES), 1) < SW_HEAD_DIM
    ka_ref[0, :, 0:LANES] = jnp.where(low2, k2, k2r).astype(BF16)
    ka_ref[1, :, 0:LANES] = jnp.where(low2, k2r, k2).astype(BF16)
    low_t = lax.broadcasted_iota(jnp.int32, (LANES, W), 0) < SW_HEAD_DIM
    for hp in range(SW_HEADS // 2):
        g, hl = (2 * hp) // hpg, (2 * hp) % hpg
        qt = qg_buf[rd2, :, hp * LANES:(hp + 1) * LANES].astype(F32).T
        qq_ref[g, 0:LANES, hl * W:(hl + 1) * W] = jnp.where(low_t, qt, 0.0).astype(BF16)
        qq_ref[g, 0:LANES, (hl + 1) * W:(hl + 2) * W] = jnp.where(low_t, 0.0, qt).astype(BF16)
    mask = mask_ref[jnp.where(first_window, 1, 0)]
    scores = [jnp.dot(ka_ref[g], qq_ref[g], preferred_element_type=F32)
              for g in range(SW_KV_HEADS)]

    x1 = x_ref[...] + proj_a + proj_b
    x1_buf[wr2] = x1
    ms = jnp.mean(x1 * x1, axis=-1, keepdims=True)
    hn = (x1 * lax.rsqrt(ms + EPS) * nw_ref[...]).astype(BF16)

    def project(j):
        sl = slice(j * BACK_TN, (j + 1) * BACK_TN)
        acc = jnp.dot(hn, win_ref[:, sl], preferred_element_type=F32) + bin_ref[:, sl]
        if (j + 1) * BACK_TN <= SW_WIDTH:
            acc = acc * SW_QSCALE
        if (j + 1) * BACK_TN <= 2 * SW_WIDTH:
            qg_buf[wr2, :, sl] = acc.astype(BF16)
        else:
            kv_buf[wr3] = acc.astype(BF16)

    n_proj = ODD_IN // BACK_TN
    done = 0

    for g in range(SW_KV_HEADS):
        probs, inv = [], []
        for hl in range(hpg):
            h = g * hpg + hl
            s = mask + scores[g][:, hl * W:(hl + 1) * W]
            sink2 = sink_ref[0:1, h:h + 1] * LOG2E
            m = jnp.maximum(jnp.max(s, axis=0, keepdims=True), sink2)
            p = jnp.exp2(s - m)
            inv.append(1.0 / (jnp.sum(p, axis=0, keepdims=True) + jnp.exp2(sink2 - m)))
            probs.append(p.astype(BF16))
            if done < n_proj:
                project(done)
                done += 1
        pt = jnp.concatenate(probs, axis=1)
        ot = lax.dot_general(vblk, pt, (((0,), (0,)), ((), ())), preferred_element_type=F32)
        ot = ot[g * SW_HEAD_DIM:(g + 1) * SW_HEAD_DIM, :] * jnp.concatenate(inv, axis=1)
        for k in range(hpg // 2):
            pair = jnp.concatenate([ot[:, (2 * k) * W:(2 * k + 1) * W],
                                    ot[:, (2 * k + 1) * W:(2 * k + 2) * W]], axis=0)
            hs = slice((g * hpg // 2 + k) * LANES, (g * hpg // 2 + k + 1) * LANES)
            gate = qg_buf[rd2, :, SW_WIDTH + hs.start:SW_WIDTH + hs.stop].astype(F32)
            gated_ref[:, hs] = (pair.T * _silu(gate)).astype(BF16)
    while done < n_proj:
        project(done)
        done += 1

    x2 = x1_buf[rd2] + jnp.dot(gated_ref[...], wo_ref[...], preferred_element_type=F32)
    ms2 = jnp.mean(x2 * x2, axis=-1, keepdims=True)
    o_ref[...] = x2 * lax.rsqrt(ms2 + EPS) * fw_ref[...]


def _back(x2, ya, yb, w_a, w_b, norm_w, w_in, b_in, sinks, w_out, final_w, seq_len):
    m = x2.shape[0]
    W = WINDOW
    nb = m // W
    cur = lambda i: (jnp.minimum(i, nb - 1), 0)
    kern = functools.partial(_back_kernel, seq_blocks=seq_len // W)
    return pl.pallas_call(
        kern,
        grid=(nb + 1,),
        in_specs=[
            pl.BlockSpec((W, D_MODEL), cur),
            pl.BlockSpec((W, DA_WIDTH), cur),
            pl.BlockSpec((W, SSD_WIDTH), cur),
            _const_spec((DA_WIDTH, D_MODEL)),
            _const_spec((SSD_WIDTH, D_MODEL)),
            _const_spec((1, D_MODEL)),
            _const_spec((D_MODEL, ODD_IN)),
            _const_spec((1, ODD_IN)),
            _const_spec((1, LANES)),
            _const_spec((SW_WIDTH, D_MODEL)),
            _const_spec((1, D_MODEL)),
        ],
        out_specs=pl.BlockSpec((W, D_MODEL), lambda i: (jnp.maximum(i - 1, 0), 0)),
        out_shape=jax.ShapeDtypeStruct((m, D_MODEL), F32),
        scratch_shapes=[
            pltpu.VMEM((2, W, D_MODEL), F32),
            pltpu.VMEM((2, W, 2 * SW_WIDTH), BF16),
            pltpu.VMEM((3, W, 2 * LANES), BF16),
            pltpu.VMEM((SW_KV_HEADS, 2 * LANES, SW_GQ * W), BF16),
            pltpu.VMEM((SW_KV_HEADS, 2 * W, 2 * LANES), BF16),
            pltpu.VMEM((2, 2 * W, W), F32),
            pltpu.VMEM((W, SW_WIDTH), BF16),
        ],
        compiler_params=_params("arbitrary"),
        name="back",
    )(x2, ya, yb, w_a, w_b, norm_w, w_in, b_in, sinks, w_out, final_w)


def _even_layer(x, norm_w, w_in, conv_w, conv_b, dt_bias, a_log, d_skip, ssd_norm_w,
                lq1, lk1, lq2, lk2, subln_w, lambda_init):
    b, s, d = x.shape
    w_main = w_in[:, :EVEN_MAIN].astype(BF16)
    w_dt = jnp.pad(w_in[:, EVEN_MAIN:], ((0, 0), (0, LANES - SSD_HEADS))).astype(BF16)
    pad = lambda v: jnp.pad(v.astype(F32), (0, LANES - SSD_HEADS)).reshape(1, LANES)
    expand = (jnp.arange(LANES)[:, None] == (jnp.arange(SSD_WIDTH)[None, :] // SSD_HEAD_DIM)).astype(BF16)
    proj, y_b = _even_front(x.reshape(b * s, d), norm_w.reshape(1, d), w_main, w_dt,
                            conv_w[:, :SSD_WIDTH], conv_b[:SSD_WIDTH].reshape(1, -1),
                            conv_w[:, SSD_WIDTH:], conv_b[SSD_WIDTH:].reshape(1, -1),
                            pad(dt_bias), pad(a_log),
                            jnp.repeat(d_skip.astype(F32), SSD_HEAD_DIM).reshape(1, -1),
                            ssd_norm_w.reshape(1, -1).astype(F32), expand, s)
    proj = proj.reshape(b, s, ATTN_COLS)

    lam_params = jnp.stack([lq1, lk1, lq2, lk2]).astype(F32)
    y_a = _diff_attn(proj, lam_params, subln_w.reshape(1, -1).astype(F32), lambda_init)
    return y_a, y_b


def kernel(x, norm_a, w_in_a, conv_w_a, conv_b_a, dt_bias_a, a_log_a, d_skip_a, ssd_norm_a,
           lambda_q1_a, lambda_k1_a, lambda_q2_a, lambda_k2_a, subln_a, w_out_a,
           norm_c, w_in_c, b_in_c, sinks_c, w_out_c, final_norm):
    b, s, d = x.shape
    lambda_init = 0.8 - 0.6 * math.exp(-0.3 * 0)
    y_a, y_b = _even_layer(x, norm_a[0], w_in_a[0], conv_w_a[0], conv_b_a[0], dt_bias_a[0], a_log_a[0],
                           d_skip_a[0], ssd_norm_a[0], lambda_q1_a[0], lambda_k1_a[0], lambda_q2_a[0],
                           lambda_k2_a[0], subln_a[0], lambda_init)

    kv0 = SW_WIDTH
    g0 = SW_WIDTH + 2 * SW_KV_HEADS * SW_HEAD_DIM
    perm = lambda w: jnp.concatenate([w[..., :kv0], w[..., g0:], w[..., kv0:g0]], axis=-1)
    w_in = perm(w_in_c[0]).astype(BF16)
    b_in = perm(b_in_c[0]).reshape(1, -1).astype(F32)
    w_out = w_out_a[0].astype(BF16)
    sinks = jnp.pad(sinks_c[0].astype(F32), (0, LANES - SW_HEADS)).reshape(1, LANES)
    out = _back(x.reshape(b * s, d), y_a.reshape(b * s, -1), y_b.reshape(b * s, -1),
                w_out[:DA_WIDTH], w_out[DA_WIDTH:], norm_c[0].reshape(1, d), w_in, b_in, sinks,
                w_out_c[0].astype(BF16), final_norm.reshape(1, d), s)
    return out.reshape(b, s, d)
```

```python
import functools
import math

import jax
import jax.numpy as jnp
from jax import lax
from jax.experimental import pallas as pl
from jax.experimental.pallas import tpu as pltpu

F32 = jnp.float32
BF16 = jnp.bfloat16

D_MODEL = 1024
EPS = 1e-5

DA_HEADS = 8
DA_HEAD_DIM = 64
DA_WIDTH = DA_HEADS * 2 * DA_HEAD_DIM

SSD_WIDTH = 1024
SSD_HEAD_DIM = 64
SSD_HEADS = SSD_WIDTH // SSD_HEAD_DIM
SSD_GROUPS = 2
SSD_STATE = 128
SSD_CONV = 4
SSD_CHUNK = 128
SSD_BC = 2 * SSD_GROUPS * SSD_STATE

SW_HEADS = 16
SW_KV_HEADS = 2
SW_GQ = SW_HEADS // SW_KV_HEADS
SW_HEAD_DIM = 64
SW_WIDTH = SW_HEADS * SW_HEAD_DIM
WINDOW = 128

EVEN_MAIN = 4 * DA_WIDTH + SSD_WIDTH + SSD_WIDTH + SSD_BC
LANES = 128
TAIL = 8
NEG = -1e30
LOG2E = math.log2(math.e)
DA_QSCALE = DA_HEAD_DIM ** -0.5 * LOG2E
SW_QSCALE = SW_HEAD_DIM ** -0.5 * LOG2E

VMEM_LIMIT = 56 * 1024 * 1024


def _alibi_slope(i, n):
    return float(2.0 ** (-8.0 * (i + 1) / n))


def _sigmoid(x):
    return 1.0 / (1.0 + jnp.exp(-x))


def _silu(x):
    return x * _sigmoid(x)


def _params(*sem):
    return pltpu.CompilerParams(dimension_semantics=sem, vmem_limit_bytes=VMEM_LIMIT)


def _const_spec(shape):
    nd = len(shape)
    return pl.BlockSpec(shape, lambda *_: (0,) * nd, pipeline_mode=pl.Buffered(1))


DA_TQ = 256
DA_AHEAD = 2
N_SPLIT = 3


def _piece_select(x, lane):
    hi, mid, lo = _split3(x)
    zero = jnp.zeros_like(hi)
    return jnp.where(lane < 2, hi, jnp.where(lane < 4, mid, jnp.where(lane < 2 * N_SPLIT, lo, zero)))


def _diff_attn_kernel(lam_ref, sub_ref, q_ref, k_ref, v_ref, g_ref, o_ref,
                      qa_ref, kaug_ref, dmask_ref, m_ref, l_ref, acc_ref, sbuf_ref, *, lambda_init):
    tq = DA_TQ
    tk = DA_TQ
    qi = pl.program_id(1)

    lp = lam_ref[...]
    lam = (jnp.exp(jnp.sum(lp[0:1] * lp[1:2], axis=-1, keepdims=True))
           - jnp.exp(jnp.sum(lp[2:3] * lp[3:4], axis=-1, keepdims=True)) + lambda_init)

    lane = lax.broadcasted_iota(jnp.int32, (tq, LANES), 1)
    rowf = lax.broadcasted_iota(jnp.int32, (tq, LANES), 0).astype(F32)
    even = (lane & 1) == 0

    @pl.when(qi == 0)
    def _():
        krow = lax.broadcasted_iota(jnp.int32, (tk, 2 * tq), 0)
        qcol = lax.broadcasted_iota(jnp.int32, (tk, 2 * tq), 1)
        qcol = jnp.where(qcol >= tq, qcol - tq, qcol)
        dmask_ref[...] = jnp.where(krow <= qcol, 0.0, NEG)
        kext = jnp.where(lane < 2 * N_SPLIT, jnp.where(even, rowf, 1.0), 0.0).astype(BF16)
        qext_t = _piece_select(jnp.where(even, LOG2E, -LOG2E * rowf), lane).astype(F32).T
        for h in range(DA_HEADS):
            qext_h = (qext_t * _alibi_slope(h, DA_HEADS)).astype(BF16)
            qa_ref[h, LANES:2 * LANES, 0:tq] = qext_h
            qa_ref[h, LANES:2 * LANES, tq:2 * tq] = qext_h
        for t in range(k_ref.shape[1] // tk):
            rows = slice(t * tk, (t + 1) * tk)
            for h in range(DA_HEADS):
                kaug_ref[rows, 2 * h * LANES:(2 * h + 1) * LANES] = k_ref[0, rows, h * LANES:(h + 1) * LANES]
                kaug_ref[rows, (2 * h + 1) * LANES:(2 * h + 2) * LANES] = kext

    first_map = lax.broadcasted_iota(jnp.int32, (LANES, tq), 0) < DA_HEAD_DIM
    for h in range(DA_HEADS):
        qt = q_ref[0, :, h * LANES:(h + 1) * LANES].astype(F32).T
        qa_ref[h, 0:LANES, 0:tq] = jnp.where(first_map, qt, 0.0).astype(BF16)
        qa_ref[h, 0:LANES, tq:2 * tq] = jnp.where(first_map, 0.0, qt).astype(BF16)

    def scores(j, h):
        kst = pl.multiple_of(j * tk, tk)
        ka = kaug_ref[pl.ds(kst, tk), 2 * h * LANES:(2 * h + 2) * LANES]
        return jnp.dot(ka, qa_ref[h], preferred_element_type=F32)

    def tile(j, diagonal):
        kst = pl.multiple_of(j * tk, tk)
        j_next = jnp.maximum(j - 1, 0)
        pending = [sbuf_ref[i] for i in range(DA_AHEAD)]
        for h in range(DA_HEADS):
            hs = slice(h * LANES, (h + 1) * LANES)
            s = pending.pop(0)
            if h + DA_AHEAD < DA_HEADS:
                pending.append(scores(j, h + DA_AHEAD))
            else:
                sbuf_ref[h + DA_AHEAD - DA_HEADS] = scores(j_next, h + DA_AHEAD - DA_HEADS)
            v = v_ref[0, pl.ds(kst, tk), hs]
            if diagonal:
                s = dmask_ref[...] + s
            tmax = jnp.max(s, axis=0, keepdims=True)
            if diagonal:
                m_new = tmax
                p = jnp.exp2(s - m_new)
                l_ref[h] = jnp.sum(p, axis=0, keepdims=True)
                pv = lax.dot_general(v, p.astype(BF16), (((0,), (0,)), ((), ())),
                                     preferred_element_type=F32)
                acc_ref[h] = pv
            else:
                m_old = m_ref[h] + _alibi_slope(h, DA_HEADS) * LOG2E * tk
                m_new = jnp.maximum(m_old, tmax)
                alpha = jnp.exp2(m_old - m_new)
                p = jnp.exp2(s - m_new)
                l_ref[h] = alpha * l_ref[h] + jnp.sum(p, axis=0, keepdims=True)
                pv = lax.dot_general(v, p.astype(BF16), (((0,), (0,)), ((), ())),
                                     preferred_element_type=F32)
                acc_ref[h] = alpha * acc_ref[h] + pv
            m_ref[h] = m_new

    for i in range(DA_AHEAD):
        sbuf_ref[i] = scores(qi, i)

    def diag_body(j, carry):
        tile(j, True)
        return carry

    lax.fori_loop(qi, qi + 1, diag_body, 0)

    def body(i, carry):
        tile(qi - 1 - i, False)
        return carry

    lax.fori_loop(0, qi, body, 0)

    post = sub_ref[...] * (1.0 - lambda_init)
    for h in range(DA_HEADS):
        hs = slice(h * LANES, (h + 1) * LANES)
        o = acc_ref[h] * (1.0 / l_ref[h])
        attn = o[:, 0:tq] - lam * o[:, tq:2 * tq]
        y = attn * lax.rsqrt(jnp.mean(attn * attn, axis=0, keepdims=True) + EPS)
        g = g_ref[0, :, hs].astype(F32)
        o_ref[0, :, hs] = (y.T * post * _silu(g)).astype(BF16)


def _diff_attn(proj, lam_params, subln_w, lambda_init):
    b, s, _ = proj.shape
    tq = DA_TQ
    kern = functools.partial(_diff_attn_kernel, lambda_init=lambda_init)
    return pl.pallas_call(
        kern,
        grid=(b, s // tq),
        in_specs=[
            _const_spec((4, DA_HEAD_DIM)),
            _const_spec((1, 2 * DA_HEAD_DIM)),
            pl.BlockSpec((1, tq, DA_WIDTH), lambda bi, qi: (bi, qi, 0)),
            pl.BlockSpec((1, s, DA_WIDTH), lambda bi, qi: (bi, 0, 1)),
            pl.BlockSpec((1, s, DA_WIDTH), lambda bi, qi: (bi, 0, 2)),
            pl.BlockSpec((1, tq, DA_WIDTH), lambda bi, qi: (bi, qi, 3)),
        ],
        out_specs=pl.BlockSpec((1, tq, DA_WIDTH), lambda bi, qi: (bi, qi, 0)),
        out_shape=jax.ShapeDtypeStruct((b, s, DA_WIDTH), BF16),
        scratch_shapes=[
            pltpu.VMEM((DA_HEADS, 2 * LANES, 2 * tq), BF16),
            pltpu.VMEM((s, 2 * DA_WIDTH), BF16),
            pltpu.VMEM((tq, 2 * tq), F32),
            pltpu.VMEM((DA_HEADS, 1, 2 * tq), F32),
            pltpu.VMEM((DA_HEADS, 1, 2 * tq), F32),
            pltpu.VMEM((DA_HEADS, LANES, 2 * tq), F32),
            pltpu.VMEM((DA_AHEAD, tq, 2 * tq), F32),
        ],
        compiler_params=_params("arbitrary", "arbitrary"),
        name="diff_attn",
    )(lam_params, subln_w, proj, proj, proj, proj)


def _split3(x):
    hi = x.astype(BF16)
    r1 = x - hi.astype(F32)
    mid = r1.astype(BF16)
    lo = (r1 - mid.astype(F32)).astype(BF16)
    return hi, mid, lo


def _expand_heads(x):
    rows = x.shape[0]
    return jnp.concatenate([jnp.broadcast_to(x[:, j:j + 1], (rows, SSD_HEAD_DIM)) for j in range(SSD_HEADS)],
                           axis=1)


def _conv_silu(raw, tail_ref, w_ref, b_ref, keep):
    rows = raw.shape[0]
    tail = tail_ref[...]
    if keep is not None:
        tail = jnp.where(keep, tail, 0.0)
    ext = jnp.concatenate([tail, raw], axis=0)
    tail_ref[...] = raw[rows - TAIL:rows, :]
    acc = b_ref[...] + w_ref[SSD_CONV - 1:SSD_CONV, :] * raw
    for d in range(1, SSD_CONV):
        shifted = pltpu.roll(ext, d, axis=0)[TAIL:TAIL + rows, :]
        acc = acc + w_ref[SSD_CONV - 1 - d:SSD_CONV - d, :] * shifted
    return _silu(acc)


def _ssd_chunk(z_raw, xs_raw, bc_raw, dt_raw, keep, params, tailx_ref, tailb_ref, state_ref):
    cwx_ref, cbx_ref, cwb_ref, cbb_ref, dtb_ref, alog_ref, dskip_ref, nw_ref = params
    L = SSD_CHUNK
    hpg = SSD_HEADS // SSD_GROUPS
    gw = SSD_WIDTH // SSD_GROUPS
    nt = (((1,), (1,)), ((), ()))
    tn = (((0,), (0,)), ((), ()))

    draw = dt_raw + dtb_ref[...]
    dt = jnp.maximum(draw, 0.0) + jnp.log1p(jnp.exp(-jnp.abs(draw)))
    a = -jnp.exp(alog_ref[...])
    da = dt * a
    row = lax.broadcasted_iota(jnp.int32, (L, L), 0)
    col = lax.broadcasted_iota(jnp.int32, (L, L), 1)
    tril = row >= col
    hi, mid, lo = _split3(da)
    ones = tril.astype(BF16)
    yield None
    cs = (jnp.dot(ones, hi, preferred_element_type=F32) + jnp.dot(ones, mid, preferred_element_type=F32)
          + jnp.dot(ones, lo, preferred_element_type=F32))
    dt_e = _expand_heads(dt)

    xs = _conv_silu(xs_raw.astype(F32), tailx_ref, cwx_ref, cbx_ref, keep)
    bc = _conv_silu(bc_raw.astype(F32), tailb_ref, cwb_ref, cbb_ref, keep)
    bcb = bc.astype(BF16)
    cs_t = cs.T
    yield None
    cs_e = _expand_heads(cs)
    bgs = [bcb[:, g * SSD_STATE:(g + 1) * SSD_STATE] for g in range(SSD_GROUPS)]
    cgs = [bcb[:, (SSD_GROUPS + g) * SSD_STATE:(SSD_GROUPS + g + 1) * SSD_STATE] for g in range(SSD_GROUPS)]
    cbs = [lax.dot_general(cgs[g], bgs[g], nt, preferred_element_type=F32) for g in range(SSD_GROUPS)]

    cs_last_e = cs_e[L - 1:L, :]
    xdt = xs * dt_e
    xdt_b = xdt.astype(BF16)
    w_state = (xdt * jnp.exp(cs_last_e - cs_e)).astype(BF16)
    e_in = jnp.exp(cs_e)
    chunk_decay = jnp.exp(cs_last_e)

    lane = lax.broadcasted_iota(jnp.int32, (L, LANES), 1)
    first_head = lane < SSD_HEAD_DIM
    zeros_b = jnp.zeros((L, LANES), BF16)

    y_diag, y_off = [], []
    for g in range(SSD_GROUPS):
        for jp in range(hpg // 2):
            pair = None
            for half in range(2):
                hh = g * hpg + 2 * jp + half
                seg = cs[:, hh:hh + 1] - cs_t[hh:hh + 1, :]
                lm = jnp.exp(jnp.where(tril, seg, NEG))
                mj = (cbs[g] * lm).astype(BF16)
                xp = xdt_b[:, (g * hpg + 2 * jp) * SSD_HEAD_DIM:(g * hpg + 2 * jp + 2) * SSD_HEAD_DIM]
                xp = jnp.where(first_head, xp, zeros_b) if half == 0 else jnp.where(first_head, zeros_b, xp)
                yield None
                t = jnp.dot(mj, xp, preferred_element_type=F32)
                pair = t if pair is None else pair + t
            y_diag.append(pair)
        gs = slice(g * gw, (g + 1) * gw)
        prev = state_ref[g]
        if keep is not None:
            prev = jnp.where(keep, prev, 0.0)
        yield None
        y_off.append(jnp.dot(cgs[g], prev.astype(BF16), preferred_element_type=F32) * e_in[:, gs])
        st = lax.dot_general(bgs[g], w_state[:, gs], tn, preferred_element_type=F32)
        state_ref[g] = prev * chunk_decay[:, gs] + st

    y = jnp.concatenate(y_diag, axis=-1) + jnp.concatenate(y_off, axis=-1) + xs * dskip_ref[...]
    y = y * _silu(z_raw.astype(F32))
    outs = []
    for g in range(SSD_GROUPS):
        yg = y[:, g * gw:(g + 1) * gw]
        outs.append(yg * lax.rsqrt(jnp.mean(yg * yg, axis=-1, keepdims=True) + EPS))
    yield (jnp.concatenate(outs, axis=-1) * nw_ref[...]).astype(BF16)


EVEN_RB = 512
EVEN_TN = 256
SSD_GAPS = 2 + SSD_HEADS + SSD_GROUPS
ATTN_COLS = 4 * DA_WIDTH
SSD_COLS = EVEN_MAIN - ATTN_COLS


def _even_front_kernel(x_ref, nw_ref, w_ref, wdt_ref, cwx_ref, cbx_ref, cwb_ref, cbb_ref, dtb_ref, alog_ref,
                       dskip_ref, snw_ref, o_ref, y_ref,
                       hn_ref, sbuf_ref, dbuf_ref, tailx_ref, tailb_ref, state_ref, *, seq_blocks):
    i = pl.program_id(0)
    L = SSD_CHUNK

    @pl.when(i == 0)
    def _():
        sbuf_ref[...] = jnp.zeros_like(sbuf_ref)
        dbuf_ref[...] = jnp.zeros_like(dbuf_ref)
        tailx_ref[...] = jnp.zeros_like(tailx_ref)
        tailb_ref[...] = jnp.zeros_like(tailb_ref)
        state_ref[...] = jnp.zeros_like(state_ref)

    wr = lax.rem(i, 2)
    rd = 1 - wr
    starts_sequence = lax.rem(i + seq_blocks - 1, seq_blocks) == 0

    x = x_ref[...]
    ms = jnp.mean(x * x, axis=-1, keepdims=True)
    hn_ref[...] = (x * lax.rsqrt(ms + EPS) * nw_ref[...]).astype(BF16)

    def project(j):
        sl = slice(j * EVEN_TN, (j + 1) * EVEN_TN)
        acc = jnp.dot(hn_ref[...], w_ref[:, sl], preferred_element_type=F32)
        if (j + 1) * EVEN_TN <= DA_WIDTH:
            acc = acc * DA_QSCALE
        if (j + 1) * EVEN_TN <= ATTN_COLS:
            o_ref[:, sl] = acc.astype(BF16)
        else:
            sbuf_ref[wr, :, j * EVEN_TN - ATTN_COLS:(j + 1) * EVEN_TN - ATTN_COLS] = acc.astype(BF16)

    n_proj = EVEN_MAIN // EVEN_TN
    n_chunks = EVEN_RB // L
    gaps = n_chunks * SSD_GAPS
    params = (cwx_ref, cbx_ref, cwb_ref, cbb_ref, dtb_ref, alog_ref, dskip_ref, snw_ref)
    done = 0
    gap = 0
    for c in range(n_chunks):
        rows = slice(c * L, (c + 1) * L)
        keep = jnp.logical_not(starts_sequence) if c == 0 else None
        for out in _ssd_chunk(sbuf_ref[rd, rows, 0:SSD_WIDTH], sbuf_ref[rd, rows, SSD_WIDTH:2 * SSD_WIDTH],
                              sbuf_ref[rd, rows, 2 * SSD_WIDTH:SSD_COLS], dbuf_ref[rd, rows, :],
                              keep, params, tailx_ref, tailb_ref, state_ref):
            if out is None:
                gap += 1
                while done < n_proj and done * gaps < gap * n_proj:
                    project(done)
                    done += 1
            else:
                y_ref[rows, :] = out
    while done < n_proj:
        project(done)
        done += 1
    dbuf_ref[wr] = jnp.dot(hn_ref[...], wdt_ref[...], preferred_element_type=F32)


def _even_front(x2, norm_w, w_main, w_dt, cw_x, cb_x, cw_b, cb_b, dt_bias, a_log, d_skip_e, ssd_norm_w, seq_len):
    m = x2.shape[0]
    rb = EVEN_RB
    nb = m // rb
    kern = functools.partial(_even_front_kernel, seq_blocks=seq_len // rb)
    return pl.pallas_call(
        kern,
        grid=(nb + 1,),
        in_specs=[
            pl.BlockSpec((rb, D_MODEL), lambda i: (jnp.minimum(i, nb - 1), 0)),
            _const_spec((1, D_MODEL)),
            _const_spec((D_MODEL, EVEN_MAIN)),
            _const_spec((D_MODEL, LANES)),
            _const_spec((SSD_CONV, SSD_WIDTH)),
            _const_spec((1, SSD_WIDTH)),
            _const_spec((SSD_CONV, SSD_BC)),
            _const_spec((1, SSD_BC)),
            _const_spec((1, LANES)),
            _const_spec((1, LANES)),
            _const_spec((1, SSD_WIDTH)),
            _const_spec((1, SSD_WIDTH)),
        ],
        out_specs=[
            pl.BlockSpec((rb, ATTN_COLS), lambda i: (jnp.minimum(i, nb - 1), 0)),
            pl.BlockSpec((rb, SSD_WIDTH), lambda i: (jnp.maximum(i - 1, 0), 0)),
        ],
        out_shape=[
            jax.ShapeDtypeStruct((m, ATTN_COLS), BF16),
            jax.ShapeDtypeStruct((m, SSD_WIDTH), BF16),
        ],
        scratch_shapes=[
            pltpu.VMEM((rb, D_MODEL), BF16),
            pltpu.VMEM((2, rb, SSD_COLS), BF16),
            pltpu.VMEM((2, rb, LANES), F32),
            pltpu.VMEM((TAIL, SSD_WIDTH), F32),
            pltpu.VMEM((TAIL, SSD_BC), F32),
            pltpu.VMEM((SSD_GROUPS, SSD_STATE, SSD_WIDTH // SSD_GROUPS), F32),
        ],
        compiler_params=_params("arbitrary"),
        name="even_front",
    )(x2, norm_w, w_main, w_dt, cw_x, cb_x, cw_b, cb_b, dt_bias, a_log, d_skip_e, ssd_norm_w)


BACK_RB = 256
BACK_TN = 256
ODD_IN = 2 * SW_WIDTH + 2 * LANES


def _back_kernel(x_ref, ya_ref, yb_ref, wa_ref, wb_ref, nw_ref, win_ref, bin_ref, sink_ref, wo_ref, fw_ref,
                 o_ref, x1_buf, qg_buf, kv_buf, qq_ref, ka_ref, mask_ref, gated_ref, *, seq_blocks):
    W = WINDOW
    nw = BACK_RB // W
    i = pl.program_id(0)
    hpg = SW_GQ

    lane = lax.broadcasted_iota(jnp.int32, (W, LANES), 1)
    even = (lane & 1) == 0

    @pl.when(i == 0)
    def _():
        x1_buf[...] = jnp.zeros_like(x1_buf)
        qg_buf[...] = jnp.zeros_like(qg_buf)
        kv_buf[...] = jnp.zeros_like(kv_buf)
        krel = lax.broadcasted_iota(jnp.int32, (2 * W, W), 0)
        qrel = lax.broadcasted_iota(jnp.int32, (2 * W, W), 1) + W
        dist = qrel - krel
        band = (dist >= 0) & (dist < W)
        mask_ref[0] = jnp.where(band, 0.0, NEG)
        mask_ref[1] = jnp.where(band & (krel >= W), 0.0, NEG)
        lane2 = lax.broadcasted_iota(jnp.int32, (2 * W, LANES), 1)
        krow = lax.broadcasted_iota(jnp.int32, (2 * W, LANES), 0).astype(F32)
        kext = jnp.where(lane2 < 2 * N_SPLIT, jnp.where((lane2 & 1) == 0, krow, 1.0), 0.0).astype(BF16)
        qrow = lax.broadcasted_iota(jnp.int32, (W, LANES), 0).astype(F32) + float(W)
        for g in range(SW_KV_HEADS):
            for w in range(nw):
                ka_ref[w, g, :, LANES:2 * LANES] = kext
            for hl in range(hpg):
                slope2 = _alibi_slope(g * hpg + hl, SW_HEADS) * LOG2E
                qext_t = _piece_select(jnp.where(even, slope2, -slope2 * qrow), lane).astype(F32).T.astype(BF16)
                for w in range(nw):
                    qq_ref[w, g, LANES:2 * LANES, hl * W:(hl + 1) * W] = qext_t

    wr2 = lax.rem(i, 2)
    rd2 = 1 - wr2
    wr3 = lax.rem(i, 3)
    own3 = lax.rem(i + 2, 3)
    prev3 = lax.rem(i + 1, 3)
    first_block = lax.rem(i + seq_blocks - 1, seq_blocks) == 0

    proj_a = jnp.dot(ya_ref[...], wa_ref[...], preferred_element_type=F32)
    proj_b = jnp.dot(yb_ref[...], wb_ref[...], preferred_element_type=F32)

    low2 = lax.broadcasted_iota(jnp.int32, (2 * W, LANES), 1) < SW_HEAD_DIM
    low_t = lax.broadcasted_iota(jnp.int32, (LANES, W), 0) < SW_HEAD_DIM
    scores, vblks, masks = [], [], []
    for w in range(nw):
        rows = slice(w * W, (w + 1) * W)
        kv_prev = kv_buf[prev3, (nw - 1) * W:nw * W, :] if w == 0 else kv_buf[own3, (w - 1) * W:w * W, :]
        kv = jnp.concatenate([kv_prev, kv_buf[own3, rows, :]], axis=0)
        k2 = kv[:, 0:LANES].astype(F32)
        vblks.append(kv[:, LANES:2 * LANES])
        k2r = pltpu.roll(k2, SW_HEAD_DIM, axis=1)
        ka_ref[w, 0, :, 0:LANES] = jnp.where(low2, k2, k2r).astype(BF16)
        ka_ref[w, 1, :, 0:LANES] = jnp.where(low2, k2r, k2).astype(BF16)
        for hp in range(SW_HEADS // 2):
            g, hl = (2 * hp) // hpg, (2 * hp) % hpg
            qt = qg_buf[rd2, rows, hp * LANES:(hp + 1) * LANES].astype(F32).T
            qq_ref[w, g, 0:LANES, hl * W:(hl + 1) * W] = jnp.where(low_t, qt, 0.0).astype(BF16)
            qq_ref[w, g, 0:LANES, (hl + 1) * W:(hl + 2) * W] = jnp.where(low_t, 0.0, qt).astype(BF16)
        masks.append(mask_ref[jnp.where(first_block, 1, 0)] if w == 0 else mask_ref[0])
        scores.append([jnp.dot(ka_ref[w, g], qq_ref[w, g], preferred_element_type=F32)
                       for g in range(SW_KV_HEADS)])

    x1 = x_ref[...] + proj_a + proj_b
    x1_buf[wr2] = x1
    ms = jnp.mean(x1 * x1, axis=-1, keepdims=True)
    hn = (x1 * lax.rsqrt(ms + EPS) * nw_ref[...]).astype(BF16)

    def project(j):
        sl = slice(j * BACK_TN, (j + 1) * BACK_TN)
        acc = jnp.dot(hn, win_ref[:, sl], preferred_element_type=F32) + bin_ref[:, sl]
        if (j + 1) * BACK_TN <= SW_WIDTH:
            acc = acc * SW_QSCALE
        if (j + 1) * BACK_TN <= 2 * SW_WIDTH:
            qg_buf[wr2, :, sl] = acc.astype(BF16)
        else:
            kv_buf[wr3] = acc.astype(BF16)

    n_proj = ODD_IN // BACK_TN
    done = 0

    for w in range(nw):
        rows = slice(w * W, (w + 1) * W)
        for g in range(SW_KV_HEADS):
            probs, inv = [], []
            for hl in range(hpg):
                h = g * hpg + hl
                s = masks[w] + scores[w][g][:, hl * W:(hl + 1) * W]
                sink2 = sink_ref[0:1, h:h + 1] * LOG2E
                m = jnp.maximum(jnp.max(s, axis=0, keepdims=True), sink2)
                p = jnp.exp2(s - m)
                inv.append(1.0 / (jnp.sum(p, axis=0, keepdims=True) + jnp.exp2(sink2 - m)))
                probs.append(p.astype(BF16))
                if done < n_proj and (hl & 1) == 1:
                    project(done)
                    done += 1
            pt = jnp.concatenate(probs, axis=1)
            ot = lax.dot_general(vblks[w], pt, (((0,), (0,)), ((), ())), preferred_element_type=F32)
            ot = ot[g * SW_HEAD_DIM:(g + 1) * SW_HEAD_DIM, :] * jnp.concatenate(inv, axis=1)
            for k in range(hpg // 2):
                pair = jnp.concatenate([ot[:, (2 * k) * W:(2 * k + 1) * W],
                                        ot[:, (2 * k + 1) * W:(2 * k + 2) * W]], axis=0)
                hs = slice((g * hpg // 2 + k) * LANES, (g * hpg // 2 + k + 1) * LANES)
                gate = qg_buf[rd2, rows, SW_WIDTH + hs.start:SW_WIDTH + hs.stop].astype(F32)
                gated_ref[rows, hs] = (pair.T * _silu(gate)).astype(BF16)
    while done < n_proj:
        project(done)
        done += 1

    x2 = x1_buf[rd2] + jnp.dot(gated_ref[...], wo_ref[...], preferred_element_type=F32)
    ms2 = jnp.mean(x2 * x2, axis=-1, keepdims=True)
    o_ref[...] = x2 * lax.rsqrt(ms2 + EPS) * fw_ref[...]


def _back(x2, ya, yb, w_a, w_b, norm_w, w_in, b_in, sinks, w_out, final_w, seq_len):
    m = x2.shape[0]
    W = WINDOW
    rb = BACK_RB
    nb = m // rb
    cur = lambda i: (jnp.minimum(i, nb - 1), 0)
    kern = functools.partial(_back_kernel, seq_blocks=seq_len // rb)
    return pl.pallas_call(
        kern,
        grid=(nb + 1,),
        in_specs=[
            pl.BlockSpec((rb, D_MODEL), cur),
            pl.BlockSpec((rb, DA_WIDTH), cur),
            pl.BlockSpec((rb, SSD_WIDTH), cur),
            _const_spec((DA_WIDTH, D_MODEL)),
            _const_spec((SSD_WIDTH, D_MODEL)),
            _const_spec((1, D_MODEL)),
            _const_spec((D_MODEL, ODD_IN)),
            _const_spec((1, ODD_IN)),
            _const_spec((1, LANES)),
            _const_spec((SW_WIDTH, D_MODEL)),
            _const_spec((1, D_MODEL)),
        ],
        out_specs=pl.BlockSpec((rb, D_MODEL), lambda i: (jnp.maximum(i - 1, 0), 0)),
        out_shape=jax.ShapeDtypeStruct((m, D_MODEL), F32),
        scratch_shapes=[
            pltpu.VMEM((2, rb, D_MODEL), F32),
            pltpu.VMEM((2, rb, 2 * SW_WIDTH), BF16),
            pltpu.VMEM((3, rb, 2 * LANES), BF16),
            pltpu.VMEM((rb // W, SW_KV_HEADS, 2 * LANES, SW_GQ * W), BF16),
            pltpu.VMEM((rb // W, SW_KV_HEADS, 2 * W, 2 * LANES), BF16),
            pltpu.VMEM((2, 2 * W, W), F32),
            pltpu.VMEM((rb, SW_WIDTH), BF16),
        ],
        compiler_params=_params("arbitrary"),
        name="back",
    )(x2, ya, yb, w_a, w_b, norm_w, w_in, b_in, sinks, w_out, final_w)


def _even_layer(x, norm_w, w_in, conv_w, conv_b, dt_bias, a_log, d_skip, ssd_norm_w,
                lq1, lk1, lq2, lk2, subln_w, lambda_init):
    b, s, d = x.shape
    w_main = w_in[:, :EVEN_MAIN].astype(BF16)
    w_dt = jnp.pad(w_in[:, EVEN_MAIN:], ((0, 0), (0, LANES - SSD_HEADS))).astype(BF16)
    pad = lambda v: jnp.pad(v.astype(F32), (0, LANES - SSD_HEADS)).reshape(1, LANES)
    proj, y_b = _even_front(x.reshape(b * s, d), norm_w.reshape(1, d), w_main, w_dt,
                            conv_w[:, :SSD_WIDTH], conv_b[:SSD_WIDTH].reshape(1, -1),
                            conv_w[:, SSD_WIDTH:], conv_b[SSD_WIDTH:].reshape(1, -1),
                            pad(dt_bias), pad(a_log),
                            jnp.repeat(d_skip.astype(F32), SSD_HEAD_DIM).reshape(1, -1),
                            ssd_norm_w.reshape(1, -1).astype(F32), s)
    proj = proj.reshape(b, s, ATTN_COLS)

    lam_params = jnp.stack([lq1, lk1, lq2, lk2]).astype(F32)
    y_a = _diff_attn(proj, lam_params, subln_w.reshape(1, -1).astype(F32), lambda_init)
    return y_a, y_b


def kernel(x, norm_a, w_in_a, conv_w_a, conv_b_a, dt_bias_a, a_log_a, d_skip_a, ssd_norm_a,
           lambda_q1_a, lambda_k1_a, lambda_q2_a, lambda_k2_a, subln_a, w_out_a,
           norm_c, w_in_c, b_in_c, sinks_c, w_out_c, final_norm):
    b, s, d = x.shape
    lambda_init = 0.8 - 0.6 * math.exp(-0.3 * 0)
    y_a, y_b = _even_layer(x, norm_a[0], w_in_a[0], conv_w_a[0], conv_b_a[0], dt_bias_a[0], a_log_a[0],
                           d_skip_a[0], ssd_norm_a[0], lambda_q1_a[0], lambda_k1_a[0], lambda_q2_a[0],
                           lambda_k2_a[0], subln_a[0], lambda_init)

    kv0 = SW_WIDTH
    g0 = SW_WIDTH + 2 * SW_KV_HEADS * SW_HEAD_DIM
    perm = lambda w: jnp.concatenate([w[..., :kv0], w[..., g0:], w[..., kv0:g0]], axis=-1)
    w_in = perm(w_in_c[0]).astype(BF16)
    b_in = perm(b_in_c[0]).reshape(1, -1).astype(F32)
    w_out = w_out_a[0].astype(BF16)
    sinks = jnp.pad(sinks_c[0].astype(F32), (0, LANES - SW_HEADS)).reshape(1, LANES)
    out = _back(x.reshape(b * s, d), y_a.reshape(b * s, -1), y_b.reshape(b * s, -1),
                w_out[:DA_WIDTH], w_out[DA_WIDTH:], norm_c[0].reshape(1, d), w_in, b_in, sinks,
                w_out_c[0].astype(BF16), final_norm.reshape(1, d), s)
    return out.reshape(b, s, d)
```

```python
import functools
import math

import jax
import jax.numpy as jnp
from jax import lax
from jax.experimental import pallas as pl
from jax.experimental.pallas import tpu as pltpu

F32 = jnp.float32
BF16 = jnp.bfloat16

D_MODEL = 1024
EPS = 1e-5

DA_HEADS = 8
DA_HEAD_DIM = 64
DA_WIDTH = DA_HEADS * 2 * DA_HEAD_DIM

SSD_WIDTH = 1024
SSD_HEAD_DIM = 64
SSD_HEADS = SSD_WIDTH // SSD_HEAD_DIM
SSD_GROUPS = 2
SSD_STATE = 128
SSD_CONV = 4
SSD_CHUNK = 128
SSD_BC = 2 * SSD_GROUPS * SSD_STATE

SW_HEADS = 16
SW_KV_HEADS = 2
SW_GQ = SW_HEADS // SW_KV_HEADS
SW_HEAD_DIM = 64
SW_WIDTH = SW_HEADS * SW_HEAD_DIM
WINDOW = 128

EVEN_MAIN = 4 * DA_WIDTH + SSD_WIDTH + SSD_WIDTH + SSD_BC
LANES = 128
TAIL = 8
NEG = -1e30
LOG2E = math.log2(math.e)
DA_QSCALE = DA_HEAD_DIM ** -0.5 * LOG2E
SW_QSCALE = SW_HEAD_DIM ** -0.5 * LOG2E

VMEM_LIMIT = 56 * 1024 * 1024


def _alibi_slope(i, n):
    return float(2.0 ** (-8.0 * (i + 1) / n))


def _sigmoid(x):
    return 1.0 / (1.0 + jnp.exp(-x))


def _silu(x):
    return x * _sigmoid(x)


def _params(*sem):
    return pltpu.CompilerParams(dimension_semantics=sem, vmem_limit_bytes=VMEM_LIMIT)


def _const_spec(shape):
    nd = len(shape)
    return pl.BlockSpec(shape, lambda *_: (0,) * nd, pipeline_mode=pl.Buffered(1))


DA_TQ = 256
DA_QT = 2
DA_AHEAD = 2
N_SPLIT = 3


def _piece_select(x, lane):
    hi, mid, lo = _split3(x)
    zero = jnp.zeros_like(hi)
    return jnp.where(lane < 2, hi, jnp.where(lane < 4, mid, jnp.where(lane < 2 * N_SPLIT, lo, zero)))


def _diff_attn_kernel(lam_ref, sub_ref, q_ref, k_ref, v_ref, g_ref, o_ref,
                      qa_ref, kaug_ref, dmask_ref, m_ref, l_ref, acc_ref, sbuf_ref, *, lambda_init):
    tq = DA_TQ
    tk = DA_TQ

    lp = lam_ref[...]
    lam = (jnp.exp(jnp.sum(lp[0:1] * lp[1:2], axis=-1, keepdims=True))
           - jnp.exp(jnp.sum(lp[2:3] * lp[3:4], axis=-1, keepdims=True)) + lambda_init)

    lane = lax.broadcasted_iota(jnp.int32, (tq, LANES), 1)
    rowf = lax.broadcasted_iota(jnp.int32, (tq, LANES), 0).astype(F32)
    even = (lane & 1) == 0

    @pl.when(pl.program_id(1) == 0)
    def _():
        krow = lax.broadcasted_iota(jnp.int32, (tk, 2 * tq), 0)
        qcol = lax.broadcasted_iota(jnp.int32, (tk, 2 * tq), 1)
        qcol = jnp.where(qcol >= tq, qcol - tq, qcol)
        dmask_ref[...] = jnp.where(krow <= qcol, 0.0, NEG)
        kext = jnp.where(lane < 2 * N_SPLIT, jnp.where(even, rowf, 1.0), 0.0).astype(BF16)
        qext_t = _piece_select(jnp.where(even, LOG2E, -LOG2E * rowf), lane).astype(F32).T
        for h in range(DA_HEADS):
            qext_h = (qext_t * _alibi_slope(h, DA_HEADS)).astype(BF16)
            qa_ref[h, LANES:2 * LANES, 0:tq] = qext_h
            qa_ref[h, LANES:2 * LANES, tq:2 * tq] = qext_h
        for t in range(k_ref.shape[1] // tk):
            rows = slice(t * tk, (t + 1) * tk)
            for h in range(DA_HEADS):
                kaug_ref[rows, 2 * h * LANES:(2 * h + 1) * LANES] = k_ref[0, rows, h * LANES:(h + 1) * LANES]
                kaug_ref[rows, (2 * h + 1) * LANES:(2 * h + 2) * LANES] = kext

    def q_tile(qi, rows):
        first_map = lax.broadcasted_iota(jnp.int32, (LANES, tq), 0) < DA_HEAD_DIM
        for h in range(DA_HEADS):
            qt = q_ref[0, rows, h * LANES:(h + 1) * LANES].astype(F32).T
            qa_ref[h, 0:LANES, 0:tq] = jnp.where(first_map, qt, 0.0).astype(BF16)
            qa_ref[h, 0:LANES, tq:2 * tq] = jnp.where(first_map, 0.0, qt).astype(BF16)

        def scores(j, h):
            kst = pl.multiple_of(j * tk, tk)
            ka = kaug_ref[pl.ds(kst, tk), 2 * h * LANES:(2 * h + 2) * LANES]
            return jnp.dot(ka, qa_ref[h], preferred_element_type=F32)

        def tile(j, diagonal):
            kst = pl.multiple_of(j * tk, tk)
            j_next = jnp.maximum(j - 1, 0)
            pending = [sbuf_ref[i] for i in range(DA_AHEAD)]
            for h in range(DA_HEADS):
                hs = slice(h * LANES, (h + 1) * LANES)
                s = pending.pop(0)
                if h + DA_AHEAD < DA_HEADS:
                    pending.append(scores(j, h + DA_AHEAD))
                else:
                    sbuf_ref[h + DA_AHEAD - DA_HEADS] = scores(j_next, h + DA_AHEAD - DA_HEADS)
                v = v_ref[0, pl.ds(kst, tk), hs]
                if diagonal:
                    s = dmask_ref[...] + s
                tmax = jnp.max(s, axis=0, keepdims=True)
                if diagonal:
                    m_new = tmax
                    p = jnp.exp2(s - m_new)
                    l_ref[h] = jnp.sum(p, axis=0, keepdims=True)
                    pv = lax.dot_general(v, p.astype(BF16), (((0,), (0,)), ((), ())),
                                         preferred_element_type=F32)
                    acc_ref[h] = pv
                else:
                    m_old = m_ref[h] + _alibi_slope(h, DA_HEADS) * LOG2E * tk
                    m_new = jnp.maximum(m_old, tmax)
                    alpha = jnp.exp2(m_old - m_new)
                    p = jnp.exp2(s - m_new)
                    l_ref[h] = alpha * l_ref[h] + jnp.sum(p, axis=0, keepdims=True)
                    pv = lax.dot_general(v, p.astype(BF16), (((0,), (0,)), ((), ())),
                                         preferred_element_type=F32)
                    acc_ref[h] = alpha * acc_ref[h] + pv
                m_ref[h] = m_new

        for i in range(DA_AHEAD):
            sbuf_ref[i] = scores(qi, i)

        def diag_body(j, carry):
            tile(j, True)
            return carry

        lax.fori_loop(qi, qi + 1, diag_body, 0)

        def body(i, carry):
            tile(qi - 1 - i, False)
            return carry

        lax.fori_loop(0, qi, body, 0)

        post = sub_ref[...] * (1.0 - lambda_init)
        for h in range(DA_HEADS):
            hs = slice(h * LANES, (h + 1) * LANES)
            o = acc_ref[h] * (1.0 / l_ref[h])
            attn = o[:, 0:tq] - lam * o[:, tq:2 * tq]
            y = attn * lax.rsqrt(jnp.mean(attn * attn, axis=0, keepdims=True) + EPS)
            g = g_ref[0, rows, hs].astype(F32)
            o_ref[0, rows, hs] = (y.T * post * _silu(g)).astype(BF16)

    for sub in range(DA_QT):
        q_tile(DA_QT * pl.program_id(1) + sub, slice(sub * tq, (sub + 1) * tq))


def _diff_attn(proj, lam_params, subln_w, lambda_init):
    b, s, _ = proj.shape
    tq = DA_TQ
    rb = DA_QT * tq
    kern = functools.partial(_diff_attn_kernel, lambda_init=lambda_init)
    return pl.pallas_call(
        kern,
        grid=(b, s // rb),
        in_specs=[
            _const_spec((4, DA_HEAD_DIM)),
            _const_spec((1, 2 * DA_HEAD_DIM)),
            pl.BlockSpec((1, rb, DA_WIDTH), lambda bi, qi: (bi, qi, 0)),
            pl.BlockSpec((1, s, DA_WIDTH), lambda bi, qi: (bi, 0, 1)),
            pl.BlockSpec((1, s, DA_WIDTH), lambda bi, qi: (bi, 0, 2)),
            pl.BlockSpec((1, rb, DA_WIDTH), lambda bi, qi: (bi, qi, 3)),
        ],
        out_specs=pl.BlockSpec((1, rb, DA_WIDTH), lambda bi, qi: (bi, qi, 0)),
        out_shape=jax.ShapeDtypeStruct((b, s, DA_WIDTH), BF16),
        scratch_shapes=[
            pltpu.VMEM((DA_HEADS, 2 * LANES, 2 * tq), BF16),
            pltpu.VMEM((s, 2 * DA_WIDTH), BF16),
            pltpu.VMEM((tq, 2 * tq), F32),
            pltpu.VMEM((DA_HEADS, 1, 2 * tq), F32),
            pltpu.VMEM((DA_HEADS, 1, 2 * tq), F32),
            pltpu.VMEM((DA_HEADS, LANES, 2 * tq), F32),
            pltpu.VMEM((DA_AHEAD, tq, 2 * tq), F32),
        ],
        compiler_params=_params("arbitrary", "arbitrary"),
        name="diff_attn",
    )(lam_params, subln_w, proj, proj, proj, proj)


def _split3(x):
    hi = x.astype(BF16)
    r1 = x - hi.astype(F32)
    mid = r1.astype(BF16)
    lo = (r1 - mid.astype(F32)).astype(BF16)
    return hi, mid, lo


def _expand_heads(x):
    rows = x.shape[0]
    return jnp.concatenate([jnp.broadcast_to(x[:, j:j + 1], (rows, SSD_HEAD_DIM)) for j in range(SSD_HEADS)],
                           axis=1)


def _conv_silu(raw, tail_ref, w_ref, b_ref, keep):
    rows = raw.shape[0]
    tail = tail_ref[...]
    if keep is not None:
        tail = jnp.where(keep, tail, 0.0)
    ext = jnp.concatenate([tail, raw], axis=0)
    tail_ref[...] = raw[rows - TAIL:rows, :]
    acc = b_ref[...] + w_ref[SSD_CONV - 1:SSD_CONV, :] * raw
    for d in range(1, SSD_CONV):
        shifted = pltpu.roll(ext, d, axis=0)[TAIL:TAIL + rows, :]
        acc = acc + w_ref[SSD_CONV - 1 - d:SSD_CONV - d, :] * shifted
    return _silu(acc)


def _ssd_chunk(z_raw, xs_raw, bc_raw, dt_raw, keep, params, tailx_ref, tailb_ref, state_ref):
    cwx_ref, cbx_ref, cwb_ref, cbb_ref, dtb_ref, alog_ref, dskip_ref, nw_ref = params
    L = SSD_CHUNK
    hpg = SSD_HEADS // SSD_GROUPS
    gw = SSD_WIDTH // SSD_GROUPS
    nt = (((1,), (1,)), ((), ()))
    tn = (((0,), (0,)), ((), ()))

    draw = dt_raw + dtb_ref[...]
    dt = jnp.maximum(draw, 0.0) + jnp.log1p(jnp.exp(-jnp.abs(draw)))
    a = -jnp.exp(alog_ref[...])
    da = dt * a
    row = lax.broadcasted_iota(jnp.int32, (L, L), 0)
    col = lax.broadcasted_iota(jnp.int32, (L, L), 1)
    tril = row >= col
    hi, mid, lo = _split3(da)
    ones = tril.astype(BF16)
    yield None
    cs = (jnp.dot(ones, hi, preferred_element_type=F32) + jnp.dot(ones, mid, preferred_element_type=F32)
          + jnp.dot(ones, lo, preferred_element_type=F32))
    dt_e = _expand_heads(dt)

    xs = _conv_silu(xs_raw.astype(F32), tailx_ref, cwx_ref, cbx_ref, keep)
    bc = _conv_silu(bc_raw.astype(F32), tailb_ref, cwb_ref, cbb_ref, keep)
    bcb = bc.astype(BF16)
    cs_t = cs.T
    yield None
    cs_e = _expand_heads(cs)
    bgs = [bcb[:, g * SSD_STATE:(g + 1) * SSD_STATE] for g in range(SSD_GROUPS)]
    cgs = [bcb[:, (SSD_GROUPS + g) * SSD_STATE:(SSD_GROUPS + g + 1) * SSD_STATE] for g in range(SSD_GROUPS)]
    cbs = [lax.dot_general(cgs[g], bgs[g], nt, preferred_element_type=F32) for g in range(SSD_GROUPS)]

    cs_last_e = cs_e[L - 1:L, :]
    xdt = xs * dt_e
    xdt_b = xdt.astype(BF16)
    w_state = (xdt * jnp.exp(cs_last_e - cs_e)).astype(BF16)
    e_in = jnp.exp(cs_e)
    chunk_decay = jnp.exp(cs_last_e)

    lane = lax.broadcasted_iota(jnp.int32, (L, LANES), 1)
    first_head = lane < SSD_HEAD_DIM
    zeros_b = jnp.zeros((L, LANES), BF16)

    y_diag, y_off = [], []
    for g in range(SSD_GROUPS):
        for jp in range(hpg // 2):
            pair = None
            for half in range(2):
                hh = g * hpg + 2 * jp + half
                seg = cs[:, hh:hh + 1] - cs_t[hh:hh + 1, :]
                lm = jnp.exp(jnp.where(tril, seg, NEG))
                mj = (cbs[g] * lm).astype(BF16)
                xp = xdt_b[:, (g * hpg + 2 * jp) * SSD_HEAD_DIM:(g * hpg + 2 * jp + 2) * SSD_HEAD_DIM]
                xp = jnp.where(first_head, xp, zeros_b) if half == 0 else jnp.where(first_head, zeros_b, xp)
                yield None
                t = jnp.dot(mj, xp, preferred_element_type=F32)
                pair = t if pair is None else pair + t
            y_diag.append(pair)
        gs = slice(g * gw, (g + 1) * gw)
        prev = state_ref[g]
        if keep is not None:
            prev = jnp.where(keep, prev, 0.0)
        yield None
        y_off.append(jnp.dot(cgs[g], prev.astype(BF16), preferred_element_type=F32) * e_in[:, gs])
        st = lax.dot_general(bgs[g], w_state[:, gs], tn, preferred_element_type=F32)
        state_ref[g] = prev * chunk_decay[:, gs] + st

    y = jnp.concatenate(y_diag, axis=-1) + jnp.concatenate(y_off, axis=-1) + xs * dskip_ref[...]
    y = y * _silu(z_raw.astype(F32))
    outs = []
    for g in range(SSD_GROUPS):
        yg = y[:, g * gw:(g + 1) * gw]
        outs.append(yg * lax.rsqrt(jnp.mean(yg * yg, axis=-1, keepdims=True) + EPS))
    yield (jnp.concatenate(outs, axis=-1) * nw_ref[...]).astype(BF16)


EVEN_RB = 512
EVEN_TN = 256
SSD_GAPS = 2 + SSD_HEADS + SSD_GROUPS
ATTN_COLS = 4 * DA_WIDTH
SSD_COLS = EVEN_MAIN - ATTN_COLS


def _even_front_kernel(x_ref, nw_ref, w_ref, wdt_ref, cwx_ref, cbx_ref, cwb_ref, cbb_ref, dtb_ref, alog_ref,
                       dskip_ref, snw_ref, o_ref, y_ref,
                       hn_ref, sbuf_ref, dbuf_ref, tailx_ref, tailb_ref, state_ref, *, seq_blocks):
    i = pl.program_id(0)
    L = SSD_CHUNK

    @pl.when(i == 0)
    def _():
        sbuf_ref[...] = jnp.zeros_like(sbuf_ref)
        dbuf_ref[...] = jnp.zeros_like(dbuf_ref)
        tailx_ref[...] = jnp.zeros_like(tailx_ref)
        tailb_ref[...] = jnp.zeros_like(tailb_ref)
        state_ref[...] = jnp.zeros_like(state_ref)

    wr = lax.rem(i, 2)
    rd = 1 - wr
    starts_sequence = lax.rem(i + seq_blocks - 1, seq_blocks) == 0

    x = x_ref[...]
    ms = jnp.mean(x * x, axis=-1, keepdims=True)
    hn_ref[...] = (x * lax.rsqrt(ms + EPS) * nw_ref[...]).astype(BF16)

    def project(j):
        sl = slice(j * EVEN_TN, (j + 1) * EVEN_TN)
        acc = jnp.dot(hn_ref[...], w_ref[:, sl], preferred_element_type=F32)
        if (j + 1) * EVEN_TN <= DA_WIDTH:
            acc = acc * DA_QSCALE
        if (j + 1) * EVEN_TN <= ATTN_COLS:
            o_ref[:, sl] = acc.astype(BF16)
        else:
            sbuf_ref[wr, :, j * EVEN_TN - ATTN_COLS:(j + 1) * EVEN_TN - ATTN_COLS] = acc.astype(BF16)

    n_proj = EVEN_MAIN // EVEN_TN
    n_chunks = EVEN_RB // L
    gaps = n_chunks * SSD_GAPS
    params = (cwx_ref, cbx_ref, cwb_ref, cbb_ref, dtb_ref, alog_ref, dskip_ref, snw_ref)
    done = 0
    gap = 0
    for c in range(n_chunks):
        rows = slice(c * L, (c + 1) * L)
        keep = jnp.logical_not(starts_sequence) if c == 0 else None
        for out in _ssd_chunk(sbuf_ref[rd, rows, 0:SSD_WIDTH], sbuf_ref[rd, rows, SSD_WIDTH:2 * SSD_WIDTH],
                              sbuf_ref[rd, rows, 2 * SSD_WIDTH:SSD_COLS], dbuf_ref[rd, rows, :],
                              keep, params, tailx_ref, tailb_ref, state_ref):
            if out is None:
                gap += 1
                while done < n_proj and done * gaps < gap * n_proj:
                    project(done)
                    done += 1
            else:
                y_ref[rows, :] = out
    while done < n_proj:
        project(done)
        done += 1
    dbuf_ref[wr] = jnp.dot(hn_ref[...], wdt_ref[...], preferred_element_type=F32)


def _even_front(x2, norm_w, w_main, w_dt, cw_x, cb_x, cw_b, cb_b, dt_bias, a_log, d_skip_e, ssd_norm_w, seq_len):
    m = x2.shape[0]
    rb = EVEN_RB
    nb = m // rb
    kern = functools.partial(_even_front_kernel, seq_blocks=seq_len // rb)
    return pl.pallas_call(
        kern,
        grid=(nb + 1,),
        in_specs=[
            pl.BlockSpec((rb, D_MODEL), lambda i: (jnp.minimum(i, nb - 1), 0)),
            _const_spec((1, D_MODEL)),
            _const_spec((D_MODEL, EVEN_MAIN)),
            _const_spec((D_MODEL, LANES)),
            _const_spec((SSD_CONV, SSD_WIDTH)),
            _const_spec((1, SSD_WIDTH)),
            _const_spec((SSD_CONV, SSD_BC)),
            _const_spec((1, SSD_BC)),
            _const_spec((1, LANES)),
            _const_spec((1, LANES)),
            _const_spec((1, SSD_WIDTH)),
            _const_spec((1, SSD_WIDTH)),
        ],
        out_specs=[
            pl.BlockSpec((rb, ATTN_COLS), lambda i: (jnp.minimum(i, nb - 1), 0)),
            pl.BlockSpec((rb, SSD_WIDTH), lambda i: (jnp.maximum(i - 1, 0), 0)),
        ],
        out_shape=[
            jax.ShapeDtypeStruct((m, ATTN_COLS), BF16),
            jax.ShapeDtypeStruct((m, SSD_WIDTH), BF16),
        ],
        scratch_shapes=[
            pltpu.VMEM((rb, D_MODEL), BF16),
            pltpu.VMEM((2, rb, SSD_COLS), BF16),
            pltpu.VMEM((2, rb, LANES), F32),
            pltpu.VMEM((TAIL, SSD_WIDTH), F32),
            pltpu.VMEM((TAIL, SSD_BC), F32),
            pltpu.VMEM((SSD_GROUPS, SSD_STATE, SSD_WIDTH // SSD_GROUPS), F32),
        ],
        compiler_params=_params("arbitrary"),
        name="even_front",
    )(x2, norm_w, w_main, w_dt, cw_x, cb_x, cw_b, cb_b, dt_bias, a_log, d_skip_e, ssd_norm_w)


BACK_RB = 512
BACK_TN = 256
ODD_IN = 2 * SW_WIDTH + 2 * LANES


def _back_kernel(x_ref, ya_ref, yb_ref, wa_ref, wb_ref, nw_ref, win_ref, bin_ref, sink_ref, wo_ref, fw_ref,
                 o_ref, x1_buf, qg_buf, kv_buf, qq_ref, ka_ref, mask_ref, gated_ref, *, seq_blocks):
    W = WINDOW
    nw = BACK_RB // W
    i = pl.program_id(0)
    hpg = SW_GQ

    lane = lax.broadcasted_iota(jnp.int32, (W, LANES), 1)
    even = (lane & 1) == 0

    @pl.when(i == 0)
    def _():
        x1_buf[...] = jnp.zeros_like(x1_buf)
        qg_buf[...] = jnp.zeros_like(qg_buf)
        kv_buf[...] = jnp.zeros_like(kv_buf)
        krel = lax.broadcasted_iota(jnp.int32, (2 * W, W), 0)
        qrel = lax.broadcasted_iota(jnp.int32, (2 * W, W), 1) + W
        dist = qrel - krel
        band = (dist >= 0) & (dist < W)
        mask_ref[0] = jnp.where(band, 0.0, NEG)
        mask_ref[1] = jnp.where(band & (krel >= W), 0.0, NEG)
        lane2 = lax.broadcasted_iota(jnp.int32, (2 * W, LANES), 1)
        krow = lax.broadcasted_iota(jnp.int32, (2 * W, LANES), 0).astype(F32)
        kext = jnp.where(lane2 < 2 * N_SPLIT, jnp.where((lane2 & 1) == 0, krow, 1.0), 0.0).astype(BF16)
        qrow = lax.broadcasted_iota(jnp.int32, (W, LANES), 0).astype(F32) + float(W)
        for g in range(SW_KV_HEADS):
            for w in range(nw):
                ka_ref[w, g, :, LANES:2 * LANES] = kext
            for hl in range(hpg):
                slope2 = _alibi_slope(g * hpg + hl, SW_HEADS) * LOG2E
                qext_t = _piece_select(jnp.where(even, slope2, -slope2 * qrow), lane).astype(F32).T.astype(BF16)
                for w in range(nw):
                    qq_ref[w, g, LANES:2 * LANES, hl * W:(hl + 1) * W] = qext_t

    wr2 = lax.rem(i, 2)
    rd2 = 1 - wr2
    wr3 = lax.rem(i, 3)
    own3 = lax.rem(i + 2, 3)
    prev3 = lax.rem(i + 1, 3)
    first_block = lax.rem(i + seq_blocks - 1, seq_blocks) == 0

    proj_a = jnp.dot(ya_ref[...], wa_ref[...], preferred_element_type=F32)
    proj_b = jnp.dot(yb_ref[...], wb_ref[...], preferred_element_type=F32)

    low2 = lax.broadcasted_iota(jnp.int32, (2 * W, LANES), 1) < SW_HEAD_DIM
    low_t = lax.broadcasted_iota(jnp.int32, (LANES, W), 0) < SW_HEAD_DIM
    scores, vblks, masks = [], [], []
    for w in range(nw):
        rows = slice(w * W, (w + 1) * W)
        kv_prev = kv_buf[prev3, (nw - 1) * W:nw * W, :] if w == 0 else kv_buf[own3, (w - 1) * W:w * W, :]
        kv = jnp.concatenate([kv_prev, kv_buf[own3, rows, :]], axis=0)
        k2 = kv[:, 0:LANES].astype(F32)
        vblks.append(kv[:, LANES:2 * LANES])
        k2r = pltpu.roll(k2, SW_HEAD_DIM, axis=1)
        ka_ref[w, 0, :, 0:LANES] = jnp.where(low2, k2, k2r).astype(BF16)
        ka_ref[w, 1, :, 0:LANES] = jnp.where(low2, k2r, k2).astype(BF16)
        for hp in range(SW_HEADS // 2):
            g, hl = (2 * hp) // hpg, (2 * hp) % hpg
            qt = qg_buf[rd2, rows, hp * LANES:(hp + 1) * LANES].astype(F32).T
            qq_ref[w, g, 0:LANES, hl * W:(hl + 1) * W] = jnp.where(low_t, qt, 0.0).astype(BF16)
            qq_ref[w, g, 0:LANES, (hl + 1) * W:(hl + 2) * W] = jnp.where(low_t, 0.0, qt).astype(BF16)
        masks.append(mask_ref[jnp.where(first_block, 1, 0)] if w == 0 else mask_ref[0])
        scores.append([jnp.dot(ka_ref[w, g], qq_ref[w, g], preferred_element_type=F32)
                       for g in range(SW_KV_HEADS)])

    x1 = x_ref[...] + proj_a + proj_b
    x1_buf[wr2] = x1
    ms = jnp.mean(x1 * x1, axis=-1, keepdims=True)
    hn = (x1 * lax.rsqrt(ms + EPS) * nw_ref[...]).astype(BF16)

    def project(j):
        sl = slice(j * BACK_TN, (j + 1) * BACK_TN)
        acc = jnp.dot(hn, win_ref[:, sl], preferred_element_type=F32) + bin_ref[:, sl]
        if (j + 1) * BACK_TN <= SW_WIDTH:
            acc = acc * SW_QSCALE
        if (j + 1) * BACK_TN <= 2 * SW_WIDTH:
            qg_buf[wr2, :, sl] = acc.astype(BF16)
        else:
            kv_buf[wr3] = acc.astype(BF16)

    n_proj = ODD_IN // BACK_TN
    done = 0

    for w in range(nw):
        rows = slice(w * W, (w + 1) * W)
        for g in range(SW_KV_HEADS):
            probs, inv = [], []
            for hl in range(hpg):
                h = g * hpg + hl
                s = masks[w] + scores[w][g][:, hl * W:(hl + 1) * W]
                sink2 = sink_ref[0:1, h:h + 1] * LOG2E
                m = jnp.maximum(jnp.max(s, axis=0, keepdims=True), sink2)
                p = jnp.exp2(s - m)
                inv.append(1.0 / (jnp.sum(p, axis=0, keepdims=True) + jnp.exp2(sink2 - m)))
                probs.append(p.astype(BF16))
                if done < n_proj and (hl & 1) == 1:
                    project(done)
                    done += 1
            pt = jnp.concatenate(probs, axis=1)
            ot = lax.dot_general(vblks[w], pt, (((0,), (0,)), ((), ())), preferred_element_type=F32)
            ot = ot[g * SW_HEAD_DIM:(g + 1) * SW_HEAD_DIM, :] * jnp.concatenate(inv, axis=1)
            for k in range(hpg // 2):
                pair = jnp.concatenate([ot[:, (2 * k) * W:(2 * k + 1) * W],
                                        ot[:, (2 * k + 1) * W:(2 * k + 2) * W]], axis=0)
                hs = slice((g * hpg // 2 + k) * LANES, (g * hpg // 2 + k + 1) * LANES)
                gate = qg_buf[rd2, rows, SW_WIDTH + hs.start:SW_WIDTH + hs.stop].astype(F32)
                gated_ref[rows, hs] = (pair.T * _silu(gate)).astype(BF16)
    while done < n_proj:
        project(done)
        done += 1

    x2 = x1_buf[rd2] + jnp.dot(gated_ref[...], wo_ref[...], preferred_element_type=F32)
    ms2 = jnp.mean(x2 * x2, axis=-1, keepdims=True)
    o_ref[...] = x2 * lax.rsqrt(ms2 + EPS) * fw_ref[...]


def _back(x2, ya, yb, w_a, w_b, norm_w, w_in, b_in, sinks, w_out, final_w, seq_len):
    m = x2.shape[0]
    W = WINDOW
    rb = BACK_RB
    nb = m // rb
    cur = lambda i: (jnp.minimum(i, nb - 1), 0)
    kern = functools.partial(_back_kernel, seq_blocks=seq_len // rb)
    return pl.pallas_call(
        kern,
        grid=(nb + 1,),
        in_specs=[
            pl.BlockSpec((rb, D_MODEL), cur),
            pl.BlockSpec((rb, DA_WIDTH), cur),
            pl.BlockSpec((rb, SSD_WIDTH), cur),
            _const_spec((DA_WIDTH, D_MODEL)),
            _const_spec((SSD_WIDTH, D_MODEL)),
            _const_spec((1, D_MODEL)),
            _const_spec((D_MODEL, ODD_IN)),
            _const_spec((1, ODD_IN)),
            _const_spec((1, LANES)),
            _const_spec((SW_WIDTH, D_MODEL)),
            _const_spec((1, D_MODEL)),
        ],
        out_specs=pl.BlockSpec((rb, D_MODEL), lambda i: (jnp.maximum(i - 1, 0), 0)),
        out_shape=jax.ShapeDtypeStruct((m, D_MODEL), F32),
        scratch_shapes=[
            pltpu.VMEM((2, rb, D_MODEL), F32),
            pltpu.VMEM((2, rb, 2 * SW_WIDTH), BF16),
            pltpu.VMEM((3, rb, 2 * LANES), BF16),
            pltpu.VMEM((rb // W, SW_KV_HEADS, 2 * LANES, SW_GQ * W), BF16),
            pltpu.VMEM((rb // W, SW_KV_HEADS, 2 * W, 2 * LANES), BF16),
            pltpu.VMEM((2, 2 * W, W), F32),
            pltpu.VMEM((rb, SW_WIDTH), BF16),
        ],
        compiler_params=_params("arbitrary"),
        name="back",
    )(x2, ya, yb, w_a, w_b, norm_w, w_in, b_in, sinks, w_out, final_w)


def _even_layer(x, norm_w, w_in, conv_w, conv_b, dt_bias, a_log, d_skip, ssd_norm_w,
                lq1, lk1, lq2, lk2, subln_w, lambda_init):
    b, s, d = x.shape
    w_main = w_in[:, :EVEN_MAIN].astype(BF16)
    w_dt = jnp.pad(w_in[:, EVEN_MAIN:], ((0, 0), (0, LANES - SSD_HEADS))).astype(BF16)
    pad = lambda v: jnp.pad(v.astype(F32), (0, LANES - SSD_HEADS)).reshape(1, LANES)
    proj, y_b = _even_front(x.reshape(b * s, d), norm_w.reshape(1, d), w_main, w_dt,
                            conv_w[:, :SSD_WIDTH], conv_b[:SSD_WIDTH].reshape(1, -1),
                            conv_w[:, SSD_WIDTH:], conv_b[SSD_WIDTH:].reshape(1, -1),
                            pad(dt_bias), pad(a_log),
                            jnp.repeat(d_skip.astype(F32), SSD_HEAD_DIM).reshape(1, -1),
                            ssd_norm_w.reshape(1, -1).astype(F32), s)
    proj = proj.reshape(b, s, ATTN_COLS)

    lam_params = jnp.stack([lq1, lk1, lq2, lk2]).astype(F32)
    y_a = _diff_attn(proj, lam_params, subln_w.reshape(1, -1).astype(F32), lambda_init)
    return y_a, y_b


def kernel(x, norm_a, w_in_a, conv_w_a, conv_b_a, dt_bias_a, a_log_a, d_skip_a, ssd_norm_a,
           lambda_q1_a, lambda_k1_a, lambda_q2_a, lambda_k2_a, subln_a, w_out_a,
           norm_c, w_in_c, b_in_c, sinks_c, w_out_c, final_norm):
    b, s, d = x.shape
    lambda_init = 0.8 - 0.6 * math.exp(-0.3 * 0)
    y_a, y_b = _even_layer(x, norm_a[0], w_in_a[0], conv_w_a[0], conv_b_a[0], dt_bias_a[0], a_log_a[0],
                           d_skip_a[0], ssd_norm_a[0], lambda_q1_a[0], lambda_k1_a[0], lambda_q2_a[0],
                           lambda_k2_a[0], subln_a[0], lambda_init)

    kv0 = SW_WIDTH
    g0 = SW_WIDTH + 2 * SW_KV_HEADS * SW_HEAD_DIM
    perm = lambda w: jnp.concatenate([w[..., :kv0], w[..., g0:], w[..., kv0:g0]], axis=-1)
    w_in = perm(w_in_c[0]).astype(BF16)
    b_in = perm(b_in_c[0]).reshape(1, -1).astype(F32)
    w_out = w_out_a[0].astype(BF16)
    sinks = jnp.pad(sinks_c[0].astype(F32), (0, LANES - SW_HEADS)).reshape(1, LANES)
    out = _back(x.reshape(b * s, d), y_a.reshape(b * s, -1), y_b.reshape(b * s, -1),
                w_out[:DA_WIDTH], w_out[DA_WIDTH:], norm_c[0].reshape(1, d), w_in, b_in, sinks,
                w_out_c[0].astype(BF16), final_norm.reshape(1, d), s)
    return out.reshape(b, s, d)
```

```python
import functools
import math

import jax
import jax.numpy as jnp
from jax import lax
from jax.experimental import pallas as pl
from jax.experimental.pallas import tpu as pltpu

F32 = jnp.float32
BF16 = jnp.bfloat16

D_MODEL = 1024
EPS = 1e-5

DA_HEADS = 8
DA_HEAD_DIM = 64
DA_WIDTH = DA_HEADS * 2 * DA_HEAD_DIM

SSD_WIDTH = 1024
SSD_HEAD_DIM = 64
SSD_HEADS = SSD_WIDTH // SSD_HEAD_DIM
SSD_GROUPS = 2
SSD_STATE = 128
SSD_CONV = 4
SSD_CHUNK = 128
SSD_BC = 2 * SSD_GROUPS * SSD_STATE

SW_HEADS = 16
SW_KV_HEADS = 2
SW_GQ = SW_HEADS // SW_KV_HEADS
SW_HEAD_DIM = 64
SW_WIDTH = SW_HEADS * SW_HEAD_DIM
WINDOW = 128

EVEN_MAIN = 4 * DA_WIDTH + SSD_WIDTH + SSD_WIDTH + SSD_BC
LANES = 128
TAIL = 8
NEG = -1e30
LOG2E = math.log2(math.e)
DA_QSCALE = DA_HEAD_DIM ** -0.5 * LOG2E
SW_QSCALE = SW_HEAD_DIM ** -0.5 * LOG2E

VMEM_LIMIT = 56 * 1024 * 1024


def _alibi_slope(i, n):
    return float(2.0 ** (-8.0 * (i + 1) / n))


def _sigmoid(x):
    return 1.0 / (1.0 + jnp.exp(-x))


def _silu(x):
    return x * _sigmoid(x)


def _params(*sem):
    return pltpu.CompilerParams(dimension_semantics=sem, vmem_limit_bytes=VMEM_LIMIT)


def _const_spec(shape):
    nd = len(shape)
    return pl.BlockSpec(shape, lambda *_: (0,) * nd, pipeline_mode=pl.Buffered(1))


DA_TQ = 256
DA_QT = 2
DA_AHEAD = 2
N_SPLIT = 3


def _piece_select(x, lane):
    hi, mid, lo = _split3(x)
    zero = jnp.zeros_like(hi)
    return jnp.where(lane < 2, hi, jnp.where(lane < 4, mid, jnp.where(lane < 2 * N_SPLIT, lo, zero)))


def _diff_attn_kernel(lam_ref, sub_ref, q_ref, k_ref, v_ref, g_ref, o_ref,
                      qa_ref, kaug_ref, dmask_ref, m_ref, l_ref, acc_ref, sbuf_ref, *, lambda_init):
    tq = DA_TQ
    tk = DA_TQ

    lp = lam_ref[...]
    lam = (jnp.exp(jnp.sum(lp[0:1] * lp[1:2], axis=-1, keepdims=True))
           - jnp.exp(jnp.sum(lp[2:3] * lp[3:4], axis=-1, keepdims=True)) + lambda_init)

    lane = lax.broadcasted_iota(jnp.int32, (tq, LANES), 1)
    rowf = lax.broadcasted_iota(jnp.int32, (tq, LANES), 0).astype(F32)
    even = (lane & 1) == 0

    @pl.when(pl.program_id(1) == 0)
    def _():
        krow = lax.broadcasted_iota(jnp.int32, (tk, 2 * tq), 0)
        qcol = lax.broadcasted_iota(jnp.int32, (tk, 2 * tq), 1)
        qcol = jnp.where(qcol >= tq, qcol - tq, qcol)
        dmask_ref[...] = jnp.where(krow <= qcol, 0.0, NEG)
        kext = jnp.where(lane < 2 * N_SPLIT, jnp.where(even, rowf, 1.0), 0.0).astype(BF16)
        qext_t = _piece_select(jnp.where(even, LOG2E, -LOG2E * rowf), lane).astype(F32).T
        for h in range(DA_HEADS):
            qext_h = (qext_t * _alibi_slope(h, DA_HEADS)).astype(BF16)
            qa_ref[h, LANES:2 * LANES, 0:tq] = qext_h
            qa_ref[h, LANES:2 * LANES, tq:2 * tq] = qext_h
        for t in range(k_ref.shape[1] // tk):
            rows = slice(t * tk, (t + 1) * tk)
            for h in range(DA_HEADS):
                kaug_ref[rows, 2 * h * LANES:(2 * h + 1) * LANES] = k_ref[0, rows, h * LANES:(h + 1) * LANES]
                kaug_ref[rows, (2 * h + 1) * LANES:(2 * h + 2) * LANES] = kext

    def q_tile(qi, rows):
        first_map = lax.broadcasted_iota(jnp.int32, (LANES, tq), 0) < DA_HEAD_DIM
        for h in range(DA_HEADS):
            qt = q_ref[0, rows, h * LANES:(h + 1) * LANES].astype(F32).T
            qa_ref[h, 0:LANES, 0:tq] = jnp.where(first_map, qt, 0.0).astype(BF16)
            qa_ref[h, 0:LANES, tq:2 * tq] = jnp.where(first_map, 0.0, qt).astype(BF16)

        def scores(j, h):
            kst = pl.multiple_of(j * tk, tk)
            ka = kaug_ref[pl.ds(kst, tk), 2 * h * LANES:(2 * h + 2) * LANES]
            return jnp.dot(ka, qa_ref[h], preferred_element_type=F32)

        def tile(j, diagonal):
            kst = pl.multiple_of(j * tk, tk)
            j_next = jnp.maximum(j - 1, 0)
            pending = [sbuf_ref[i] for i in range(DA_AHEAD)]
            for h in range(DA_HEADS):
                hs = slice(h * LANES, (h + 1) * LANES)
                s = pending.pop(0)
                if h + DA_AHEAD < DA_HEADS:
                    pending.append(scores(j, h + DA_AHEAD))
                else:
                    sbuf_ref[h + DA_AHEAD - DA_HEADS] = scores(j_next, h + DA_AHEAD - DA_HEADS)
                v = v_ref[0, pl.ds(kst, tk), hs]
                if diagonal:
                    s = dmask_ref[...] + s
                tmax = jnp.max(s, axis=0, keepdims=True)
                if diagonal:
                    m_new = tmax
                    p = jnp.exp2(s - m_new)
                    l_ref[h] = jnp.sum(p, axis=0, keepdims=True)
                    pv = lax.dot_general(v, p.astype(BF16), (((0,), (0,)), ((), ())),
                                         preferred_element_type=F32)
                    acc_ref[h] = pv
                else:
                    m_old = m_ref[h] + _alibi_slope(h, DA_HEADS) * LOG2E * tk
                    m_new = jnp.maximum(m_old, tmax)
                    alpha = jnp.exp2(m_old - m_new)
                    p = jnp.exp2(s - m_new)
                    l_ref[h] = alpha * l_ref[h] + jnp.sum(p, axis=0, keepdims=True)
                    pv = lax.dot_general(v, p.astype(BF16), (((0,), (0,)), ((), ())),
                                         preferred_element_type=F32)
                    acc_ref[h] = alpha * acc_ref[h] + pv
                m_ref[h] = m_new

        for i in range(DA_AHEAD):
            sbuf_ref[i] = scores(qi, i)

        def diag_body(j, carry):
            tile(j, True)
            return carry

        lax.fori_loop(qi, qi + 1, diag_body, 0)

        def body(i, carry):
            tile(qi - 1 - i, False)
            return carry

        lax.fori_loop(0, qi, body, 0)

        post = sub_ref[...] * (1.0 - lambda_init)
        for h in range(DA_HEADS):
            hs = slice(h * LANES, (h + 1) * LANES)
            o = acc_ref[h] * (1.0 / l_ref[h])
            attn = o[:, 0:tq] - lam * o[:, tq:2 * tq]
            y = attn * lax.rsqrt(jnp.mean(attn * attn, axis=0, keepdims=True) + EPS)
            g = g_ref[0, rows, hs].astype(F32)
            o_ref[0, rows, hs] = (y.T * post * _silu(g)).astype(BF16)

    for sub in range(DA_QT):
        q_tile(DA_QT * pl.program_id(1) + sub, slice(sub * tq, (sub + 1) * tq))


def _diff_attn(proj, lam_params, subln_w, lambda_init):
    b, s, _ = proj.shape
    tq = DA_TQ
    rb = DA_QT * tq
    kern = functools.partial(_diff_attn_kernel, lambda_init=lambda_init)
    return pl.pallas_call(
        kern,
        grid=(b, s // rb),
        in_specs=[
            _const_spec((4, DA_HEAD_DIM)),
            _const_spec((1, 2 * DA_HEAD_DIM)),
            pl.BlockSpec((1, rb, DA_WIDTH), lambda bi, qi: (bi, qi, 0)),
            pl.BlockSpec((1, s, DA_WIDTH), lambda bi, qi: (bi, 0, 1)),
            pl.BlockSpec((1, s, DA_WIDTH), lambda bi, qi: (bi, 0, 2)),
            pl.BlockSpec((1, rb, DA_WIDTH), lambda bi, qi: (bi, qi, 3)),
        ],
        out_specs=pl.BlockSpec((1, rb, DA_WIDTH), lambda bi, qi: (bi, qi, 0)),
        out_shape=jax.ShapeDtypeStruct((b, s, DA_WIDTH), BF16),
        scratch_shapes=[
            pltpu.VMEM((DA_HEADS, 2 * LANES, 2 * tq), BF16),
            pltpu.VMEM((s, 2 * DA_WIDTH), BF16),
            pltpu.VMEM((tq, 2 * tq), F32),
            pltpu.VMEM((DA_HEADS, 1, 2 * tq), F32),
            pltpu.VMEM((DA_HEADS, 1, 2 * tq), F32),
            pltpu.VMEM((DA_HEADS, LANES, 2 * tq), F32),
            pltpu.VMEM((DA_AHEAD, tq, 2 * tq), F32),
        ],
        compiler_params=_params("arbitrary", "arbitrary"),
        name="diff_attn",
    )(lam_params, subln_w, proj, proj, proj, proj)


def _split3(x):
    hi = x.astype(BF16)
    r1 = x - hi.astype(F32)
    mid = r1.astype(BF16)
    lo = (r1 - mid.astype(F32)).astype(BF16)
    return hi, mid, lo


def _expand_heads(x):
    rows = x.shape[0]
    return jnp.concatenate([jnp.broadcast_to(x[:, j:j + 1], (rows, SSD_HEAD_DIM)) for j in range(SSD_HEADS)],
                           axis=1)


def _conv_silu(raw, tail_ref, w_ref, b_ref, keep):
    rows = raw.shape[0]
    tail = tail_ref[...]
    if keep is not None:
        tail = jnp.where(keep, tail, 0.0)
    ext = jnp.concatenate([tail, raw], axis=0)
    tail_ref[...] = raw[rows - TAIL:rows, :]
    acc = b_ref[...] + w_ref[SSD_CONV - 1:SSD_CONV, :] * raw
    for d in range(1, SSD_CONV):
        shifted = pltpu.roll(ext, d, axis=0)[TAIL:TAIL + rows, :]
        acc = acc + w_ref[SSD_CONV - 1 - d:SSD_CONV - d, :] * shifted
    return _silu(acc)


def _ssd_chunk(z_raw, xs_raw, bc_raw, dt_raw, keep, params, tailx_ref, tailb_ref, state_ref):
    cwx_ref, cbx_ref, cwb_ref, cbb_ref, dtb_ref, alog_ref, dskip_ref, nw_ref = params
    L = SSD_CHUNK
    hpg = SSD_HEADS // SSD_GROUPS
    gw = SSD_WIDTH // SSD_GROUPS
    nt = (((1,), (1,)), ((), ()))
    tn = (((0,), (0,)), ((), ()))

    draw = dt_raw + dtb_ref[...]
    dt = jnp.maximum(draw, 0.0) + jnp.log1p(jnp.exp(-jnp.abs(draw)))
    a = -jnp.exp(alog_ref[...])
    da = dt * a
    row = lax.broadcasted_iota(jnp.int32, (L, L), 0)
    col = lax.broadcasted_iota(jnp.int32, (L, L), 1)
    tril = row >= col
    hi, mid, lo = _split3(da)
    ones = tril.astype(BF16)
    yield None
    cs = (jnp.dot(ones, hi, preferred_element_type=F32) + jnp.dot(ones, mid, preferred_element_type=F32)
          + jnp.dot(ones, lo, preferred_element_type=F32))
    dt_e = _expand_heads(dt)

    xs = _conv_silu(xs_raw.astype(F32), tailx_ref, cwx_ref, cbx_ref, keep)
    bc = _conv_silu(bc_raw.astype(F32), tailb_ref, cwb_ref, cbb_ref, keep)
    bcb = bc.astype(BF16)
    cs_t = cs.T
    yield None
    cs_e = _expand_heads(cs)
    bgs = [bcb[:, g * SSD_STATE:(g + 1) * SSD_STATE] for g in range(SSD_GROUPS)]
    cgs = [bcb[:, (SSD_GROUPS + g) * SSD_STATE:(SSD_GROUPS + g + 1) * SSD_STATE] for g in range(SSD_GROUPS)]
    cbs = [lax.dot_general(cgs[g], bgs[g], nt, preferred_element_type=F32) for g in range(SSD_GROUPS)]

    cs_last_e = cs_e[L - 1:L, :]
    xdt = xs * dt_e
    xdt_b = xdt.astype(BF16)
    w_state = (xdt * jnp.exp(cs_last_e - cs_e)).astype(BF16)
    e_in = jnp.exp(cs_e)
    chunk_decay = jnp.exp(cs_last_e)

    lane = lax.broadcasted_iota(jnp.int32, (L, LANES), 1)
    first_head = lane < SSD_HEAD_DIM
    zeros_b = jnp.zeros((L, LANES), BF16)

    y_diag, y_off = [], []
    for g in range(SSD_GROUPS):
        for jp in range(hpg // 2):
            pair = None
            for half in range(2):
                hh = g * hpg + 2 * jp + half
                seg = cs[:, hh:hh + 1] - cs_t[hh:hh + 1, :]
                lm = jnp.exp(jnp.where(tril, seg, NEG))
                mj = (cbs[g] * lm).astype(BF16)
                xp = xdt_b[:, (g * hpg + 2 * jp) * SSD_HEAD_DIM:(g * hpg + 2 * jp + 2) * SSD_HEAD_DIM]
                xp = jnp.where(first_head, xp, zeros_b) if half == 0 else jnp.where(first_head, zeros_b, xp)
                yield None
                t = jnp.dot(mj, xp, preferred_element_type=F32)
                pair = t if pair is None else pair + t
            y_diag.append(pair)
        gs = slice(g * gw, (g + 1) * gw)
        prev = state_ref[g]
        if keep is not None:
            prev = jnp.where(keep, prev, 0.0)
        yield None
        y_off.append(jnp.dot(cgs[g], prev.astype(BF16), preferred_element_type=F32) * e_in[:, gs])
        st = lax.dot_general(bgs[g], w_state[:, gs], tn, preferred_element_type=F32)
        state_ref[g] = prev * chunk_decay[:, gs] + st

    y = jnp.concatenate(y_diag, axis=-1) + jnp.concatenate(y_off, axis=-1) + xs * dskip_ref[...]
    y = y * _silu(z_raw.astype(F32))
    outs = []
    for g in range(SSD_GROUPS):
        yg = y[:, g * gw:(g + 1) * gw]
        outs.append(yg * lax.rsqrt(jnp.mean(yg * yg, axis=-1, keepdims=True) + EPS))
    yield (jnp.concatenate(outs, axis=-1) * nw_ref[...]).astype(BF16)


EVEN_RB = 512
EVEN_TN = 256
SSD_GAPS = 2 + SSD_HEADS + SSD_GROUPS
ATTN_COLS = 4 * DA_WIDTH
SSD_COLS = EVEN_MAIN - ATTN_COLS


def _even_front_kernel(x_ref, nw_ref, w_ref, wdt_ref, cwx_ref, cbx_ref, cwb_ref, cbb_ref, dtb_ref, alog_ref,
                       dskip_ref, snw_ref, o_ref, y_ref,
                       hn_ref, sbuf_ref, dbuf_ref, tailx_ref, tailb_ref, state_ref, *, seq_blocks):
    i = pl.program_id(0)
    L = SSD_CHUNK

    @pl.when(i == 0)
    def _():
        sbuf_ref[...] = jnp.zeros_like(sbuf_ref)
        dbuf_ref[...] = jnp.zeros_like(dbuf_ref)
        tailx_ref[...] = jnp.zeros_like(tailx_ref)
        tailb_ref[...] = jnp.zeros_like(tailb_ref)
        state_ref[...] = jnp.zeros_like(state_ref)

    wr = lax.rem(i, 2)
    rd = 1 - wr
    starts_sequence = lax.rem(i + seq_blocks - 1, seq_blocks) == 0

    x = x_ref[...]
    ms = jnp.mean(x * x, axis=-1, keepdims=True)
    hn_ref[...] = (x * lax.rsqrt(ms + EPS) * nw_ref[...]).astype(BF16)

    def project(j):
        sl = slice(j * EVEN_TN, (j + 1) * EVEN_TN)
        acc = jnp.dot(hn_ref[...], w_ref[:, sl], preferred_element_type=F32)
        if (j + 1) * EVEN_TN <= DA_WIDTH:
            acc = acc * DA_QSCALE
        if (j + 1) * EVEN_TN <= ATTN_COLS:
            o_ref[:, sl] = acc.astype(BF16)
        else:
            sbuf_ref[wr, :, j * EVEN_TN - ATTN_COLS:(j + 1) * EVEN_TN - ATTN_COLS] = acc.astype(BF16)

    n_proj = EVEN_MAIN // EVEN_TN
    n_chunks = EVEN_RB // L
    gaps = n_chunks * SSD_GAPS
    params = (cwx_ref, cbx_ref, cwb_ref, cbb_ref, dtb_ref, alog_ref, dskip_ref, snw_ref)
    done = 0
    gap = 0
    for c in range(n_chunks):
        rows = slice(c * L, (c + 1) * L)
        keep = jnp.logical_not(starts_sequence) if c == 0 else None
        for out in _ssd_chunk(sbuf_ref[rd, rows, 0:SSD_WIDTH], sbuf_ref[rd, rows, SSD_WIDTH:2 * SSD_WIDTH],
                              sbuf_ref[rd, rows, 2 * SSD_WIDTH:SSD_COLS], dbuf_ref[rd, rows, :],
                              keep, params, tailx_ref, tailb_ref, state_ref):
            if out is None:
                gap += 1
                while done < n_proj and done * gaps < gap * n_proj:
                    project(done)
                    done += 1
            else:
                y_ref[rows, :] = out
    while done < n_proj:
        project(done)
        done += 1
    dbuf_ref[wr] = jnp.dot(hn_ref[...], wdt_ref[...], preferred_element_type=F32)


def _even_front(x2, norm_w, w_main, w_dt, cw_x, cb_x, cw_b, cb_b, dt_bias, a_log, d_skip_e, ssd_norm_w, seq_len):
    m = x2.shape[0]
    rb = EVEN_RB
    nb = m // rb
    kern = functools.partial(_even_front_kernel, seq_blocks=seq_len // rb)
    return pl.pallas_call(
        kern,
        grid=(nb + 1,),
        in_specs=[
            pl.BlockSpec((rb, D_MODEL), lambda i: (jnp.minimum(i, nb - 1), 0)),
            _const_spec((1, D_MODEL)),
            _const_spec((D_MODEL, EVEN_MAIN)),
            _const_spec((D_MODEL, LANES)),
            _const_spec((SSD_CONV, SSD_WIDTH)),
            _const_spec((1, SSD_WIDTH)),
            _const_spec((SSD_CONV, SSD_BC)),
            _const_spec((1, SSD_BC)),
            _const_spec((1, LANES)),
            _const_spec((1, LANES)),
            _const_spec((1, SSD_WIDTH)),
            _const_spec((1, SSD_WIDTH)),
        ],
        out_specs=[
            pl.BlockSpec((rb, ATTN_COLS), lambda i: (jnp.minimum(i, nb - 1), 0)),
            pl.BlockSpec((rb, SSD_WIDTH), lambda i: (jnp.maximum(i - 1, 0), 0)),
        ],
        out_shape=[
            jax.ShapeDtypeStruct((m, ATTN_COLS), BF16),
            jax.ShapeDtypeStruct((m, SSD_WIDTH), BF16),
        ],
        scratch_shapes=[
            pltpu.VMEM((rb, D_MODEL), BF16),
            pltpu.VMEM((2, rb, SSD_COLS), BF16),
            pltpu.VMEM((2, rb, LANES), F32),
            pltpu.VMEM((TAIL, SSD_WIDTH), F32),
            pltpu.VMEM((TAIL, SSD_BC), F32),
            pltpu.VMEM((SSD_GROUPS, SSD_STATE, SSD_WIDTH // SSD_GROUPS), F32),
        ],
        compiler_params=_params("arbitrary"),
        name="even_front",
    )(x2, norm_w, w_main, w_dt, cw_x, cb_x, cw_b, cb_b, dt_bias, a_log, d_skip_e, ssd_norm_w)


BACK_RB = 512
BACK_TN = 256
ODD_IN = 2 * SW_WIDTH + 2 * LANES


def _back_kernel(x_ref, ya_ref, yb_ref, wa_ref, wb_ref, nw_ref, win_ref, bin_ref, sink_ref, wo_ref, fw_ref,
                 o_ref, x1_buf, qg_buf, kv_buf, qq_ref, ka_ref, mask_ref, gated_ref, *, seq_blocks):
    W = WINDOW
    nw = BACK_RB // W
    i = pl.program_id(0)
    hpg = SW_GQ

    lane = lax.broadcasted_iota(jnp.int32, (W, LANES), 1)
    even = (lane & 1) == 0

    @pl.when(i == 0)
    def _():
        x1_buf[...] = jnp.zeros_like(x1_buf)
        qg_buf[...] = jnp.zeros_like(qg_buf)
        kv_buf[...] = jnp.zeros_like(kv_buf)
        krel = lax.broadcasted_iota(jnp.int32, (2 * W, W), 0)
        qrel = lax.broadcasted_iota(jnp.int32, (2 * W, W), 1) + W
        dist = qrel - krel
        band = (dist >= 0) & (dist < W)
        mask_ref[0] = jnp.where(band, 0.0, NEG)
        mask_ref[1] = jnp.where(band & (krel >= W), 0.0, NEG)
        lane2 = lax.broadcasted_iota(jnp.int32, (2 * W, LANES), 1)
        krow = lax.broadcasted_iota(jnp.int32, (2 * W, LANES), 0).astype(F32)
        kext = jnp.where(lane2 < 2 * N_SPLIT, jnp.where((lane2 & 1) == 0, krow, 1.0), 0.0).astype(BF16)
        qrow = lax.broadcasted_iota(jnp.int32, (W, LANES), 0).astype(F32) + float(W)
        for g in range(SW_KV_HEADS):
            for w in range(nw):
                ka_ref[w, g, :, LANES:2 * LANES] = kext
            for hl in range(hpg):
                slope2 = _alibi_slope(g * hpg + hl, SW_HEADS) * LOG2E
                qext_t = _piece_select(jnp.where(even, slope2, -slope2 * qrow), lane).astype(F32).T.astype(BF16)
                for w in range(nw):
                    qq_ref[w, g, LANES:2 * LANES, hl * W:(hl + 1) * W] = qext_t

    wr2 = lax.rem(i, 2)
    rd2 = 1 - wr2
    wr3 = lax.rem(i, 3)
    own3 = lax.rem(i + 2, 3)
    prev3 = lax.rem(i + 1, 3)
    first_block = lax.rem(i + seq_blocks - 1, seq_blocks) == 0

    proj_a = jnp.dot(ya_ref[...], wa_ref[...], preferred_element_type=F32)
    proj_b = jnp.dot(yb_ref[...], wb_ref[...], preferred_element_type=F32)

    low2 = lax.broadcasted_iota(jnp.int32, (2 * W, LANES), 1) < SW_HEAD_DIM
    low_t = lax.broadcasted_iota(jnp.int32, (LANES, W), 0) < SW_HEAD_DIM
    scores, vblks, masks = [], [], []
    for w in range(nw):
        rows = slice(w * W, (w + 1) * W)
        kv_prev = kv_buf[prev3, (nw - 1) * W:nw * W, :] if w == 0 else kv_buf[own3, (w - 1) * W:w * W, :]
        kv = jnp.concatenate([kv_prev, kv_buf[own3, rows, :]], axis=0)
        k2 = kv[:, 0:LANES].astype(F32)
        vblks.append(kv[:, LANES:2 * LANES])
        k2r = pltpu.roll(k2, SW_HEAD_DIM, axis=1)
        ka_ref[w, 0, :, 0:LANES] = jnp.where(low2, k2, k2r).astype(BF16)
        ka_ref[w, 1, :, 0:LANES] = jnp.where(low2, k2r, k2).astype(BF16)
        for hp in range(SW_HEADS // 2):
            g, hl = (2 * hp) // hpg, (2 * hp) % hpg
            qt = qg_buf[rd2, rows, hp * LANES:(hp + 1) * LANES].astype(F32).T
            qq_ref[w, g, 0:LANES, hl * W:(hl + 1) * W] = jnp.where(low_t, qt, 0.0).astype(BF16)
            qq_ref[w, g, 0:LANES, (hl + 1) * W:(hl + 2) * W] = jnp.where(low_t, 0.0, qt).astype(BF16)
        masks.append(mask_ref[jnp.where(first_block, 1, 0)] if w == 0 else mask_ref[0])
        scores.append([jnp.dot(ka_ref[w, g], qq_ref[w, g], preferred_element_type=F32)
                       for g in range(SW_KV_HEADS)])

    x1 = x_ref[...] + proj_a + proj_b
    x1_buf[wr2] = x1
    ms = jnp.mean(x1 * x1, axis=-1, keepdims=True)
    hn = (x1 * lax.rsqrt(ms + EPS) * nw_ref[...]).astype(BF16)

    def project(j):
        sl = slice(j * BACK_TN, (j + 1) * BACK_TN)
        acc = jnp.dot(hn, win_ref[:, sl], preferred_element_type=F32) + bin_ref[:, sl]
        if (j + 1) * BACK_TN <= SW_WIDTH:
            acc = acc * SW_QSCALE
        if (j + 1) * BACK_TN <= 2 * SW_WIDTH:
            qg_buf[wr2, :, sl] = acc.astype(BF16)
        else:
            kv_buf[wr3] = acc.astype(BF16)

    n_proj = ODD_IN // BACK_TN
    done = 0

    for w in range(nw):
        rows = slice(w * W, (w + 1) * W)
        for g in range(SW_KV_HEADS):
            probs, inv = [], []
            for hl in range(hpg):
                h = g * hpg + hl
                s = masks[w] + scores[w][g][:, hl * W:(hl + 1) * W]
                sink2 = sink_ref[0:1, h:h + 1] * LOG2E
                m = jnp.maximum(jnp.max(s, axis=0, keepdims=True), sink2)
                p = jnp.exp2(s - m)
                inv.append(1.0 / (jnp.sum(p, axis=0, keepdims=True) + jnp.exp2(sink2 - m)))
                probs.append(p.astype(BF16))
                if done < n_proj and (hl & 1) == 1:
                    project(done)
                    done += 1
            pt = jnp.concatenate(probs, axis=1)
            ot = lax.dot_general(vblks[w], pt, (((0,), (0,)), ((), ())), preferred_element_type=F32)
            ot = ot[g * SW_HEAD_DIM:(g + 1) * SW_HEAD_DIM, :] * jnp.concatenate(inv, axis=1)
            for k in range(hpg // 2):
                pair = jnp.concatenate([ot[:, (2 * k) * W:(2 * k + 1) * W],
                                        ot[:, (2 * k + 1) * W:(2 * k + 2) * W]], axis=0)
                hs = slice((g * hpg // 2 + k) * LANES, (g * hpg // 2 + k + 1) * LANES)
                gate = qg_buf[rd2, rows, SW_WIDTH + hs.start:SW_WIDTH + hs.stop].astype(F32)
                gated_ref[rows, hs] = (pair.T * _silu(gate)).astype(BF16)
    while done < n_proj:
        project(done)
        done += 1

    x2 = x1_buf[rd2] + jnp.dot(gated_ref[...], wo_ref[...], preferred_element_type=F32)
    ms2 = jnp.mean(x2 * x2, axis=-1, keepdims=True)
    o_ref[...] = x2 * lax.rsqrt(ms2 + EPS) * fw_ref[...]


def _back(x2, ya, yb, w_a, w_b, norm_w, w_in, b_in, sinks, w_out, final_w, seq_len):
    m = x2.shape[0]
    W = WINDOW
    rb = BACK_RB
    nb = m // rb
    cur = lambda i: (jnp.minimum(i, nb - 1), 0)
    kern = functools.partial(_back_kernel, seq_blocks=seq_len // rb)
    return pl.pallas_call(
        kern,
        grid=(nb + 1,),
        in_specs=[
            pl.BlockSpec((rb, D_MODEL), cur),
            pl.BlockSpec((rb, DA_WIDTH), cur),
            pl.BlockSpec((rb, SSD_WIDTH), cur),
            _const_spec((DA_WIDTH, D_MODEL)),
            _const_spec((SSD_WIDTH, D_MODEL)),
            _const_spec((1, D_MODEL)),
            _const_spec((D_MODEL, ODD_IN)),
            _const_spec((1, ODD_IN)),
            _const_spec((1, LANES)),
            _const_spec((SW_WIDTH, D_MODEL)),
            _const_spec((1, D_MODEL)),
        ],
        out_specs=pl.BlockSpec((rb, D_MODEL), lambda i: (jnp.maximum(i - 1, 0), 0)),
        out_shape=jax.ShapeDtypeStruct((m, D_MODEL), F32),
        scratch_shapes=[
            pltpu.VMEM((2, rb, D_MODEL), F32),
            pltpu.VMEM((2, rb, 2 * SW_WIDTH), BF16),
            pltpu.VMEM((3, rb, 2 * LANES), BF16),
            pltpu.VMEM((rb // W, SW_KV_HEADS, 2 * LANES, SW_GQ * W), BF16),
            pltpu.VMEM((rb // W, SW_KV_HEADS, 2 * W, 2 * LANES), BF16),
            pltpu.VMEM((2, 2 * W, W), F32),
            pltpu.VMEM((rb, SW_WIDTH), BF16),
        ],
        compiler_params=_params("arbitrary"),
        name="back",
    )(x2, ya, yb, w_a, w_b, norm_w, w_in, b_in, sinks, w_out, final_w)


def _even_layer(x, norm_w, w_in, conv_w, conv_b, dt_bias, a_log, d_skip, ssd_norm_w,
                lq1, lk1, lq2, lk2, subln_w, lambda_init):
    b, s, d = x.shape
    w_main = w_in.astype(BF16)
    w_dt = jnp.pad(w_in[:, EVEN_MAIN:], ((0, 0), (0, LANES - SSD_HEADS))).astype(BF16)
    pad = lambda v: jnp.pad(v.astype(F32), (0, LANES - SSD_HEADS)).reshape(1, LANES)
    proj, y_b = _even_front(x.reshape(b * s, d), norm_w.reshape(1, d), w_main, w_dt,
                            conv_w[:, :SSD_WIDTH], conv_b[:SSD_WIDTH].reshape(1, -1),
                            conv_w[:, SSD_WIDTH:], conv_b[SSD_WIDTH:].reshape(1, -1),
                            pad(dt_bias), pad(a_log),
                            jnp.repeat(d_skip.astype(F32), SSD_HEAD_DIM).reshape(1, -1),
                            ssd_norm_w.reshape(1, -1).astype(F32), s)
    proj = proj.reshape(b, s, ATTN_COLS)

    lam_params = jnp.stack([lq1, lk1, lq2, lk2]).astype(F32)
    y_a = _diff_attn(proj, lam_params, subln_w.reshape(1, -1).astype(F32), lambda_init)
    return y_a, y_b


def kernel(x, norm_a, w_in_a, conv_w_a, conv_b_a, dt_bias_a, a_log_a, d_skip_a, ssd_norm_a,
           lambda_q1_a, lambda_k1_a, lambda_q2_a, lambda_k2_a, subln_a, w_out_a,
           norm_c, w_in_c, b_in_c, sinks_c, w_out_c, final_norm):
    b, s, d = x.shape
    lambda_init = 0.8 - 0.6 * math.exp(-0.3 * 0)
    y_a, y_b = _even_layer(x, norm_a[0], w_in_a[0], conv_w_a[0], conv_b_a[0], dt_bias_a[0], a_log_a[0],
                           d_skip_a[0], ssd_norm_a[0], lambda_q1_a[0], lambda_k1_a[0], lambda_q2_a[0],
                           lambda_k2_a[0], subln_a[0], lambda_init)

    kv0 = SW_WIDTH
    g0 = SW_WIDTH + 2 * SW_KV_HEADS * SW_HEAD_DIM
    perm = lambda w: jnp.concatenate([w[..., :kv0], w[..., g0:], w[..., kv0:g0]], axis=-1)
    w_in = perm(w_in_c[0]).astype(BF16)
    b_in = perm(b_in_c[0]).reshape(1, -1).astype(F32)
    w_out = w_out_a[0].astype(BF16)
    sinks = jnp.pad(sinks_c[0].astype(F32), (0, LANES - SW_HEADS)).reshape(1, LANES)
    out = _back(x.reshape(b * s, d), y_a.reshape(b * s, -1), y_b.reshape(b * s, -1),
                w_out[:DA_WIDTH], w_out[DA_WIDTH:], norm_c[0].reshape(1, d), w_in, b_in, sinks,
                w_out_c[0].astype(BF16), final_norm.reshape(1, d), s)
    return out.reshape(b, s, d)
```

```python
import functools
import math

import jax
import jax.numpy as jnp
from jax import lax
from jax.experimental import pallas as pl
from jax.experimental.pallas import tpu as pltpu

F32 = jnp.float32
BF16 = jnp.bfloat16

D_MODEL = 1024
EPS = 1e-5

DA_HEADS = 8
DA_HEAD_DIM = 64
DA_WIDTH = DA_HEADS * 2 * DA_HEAD_DIM

SSD_WIDTH = 1024
SSD_HEAD_DIM = 64
SSD_HEADS = SSD_WIDTH // SSD_HEAD_DIM
SSD_GROUPS = 2
SSD_STATE = 128
SSD_CONV = 4
SSD_CHUNK = 128
SSD_BC = 2 * SSD_GROUPS * SSD_STATE

SW_HEADS = 16
SW_KV_HEADS = 2
SW_GQ = SW_HEADS // SW_KV_HEADS
SW_HEAD_DIM = 64
SW_WIDTH = SW_HEADS * SW_HEAD_DIM
WINDOW = 128

EVEN_MAIN = 4 * DA_WIDTH + SSD_WIDTH + SSD_WIDTH + SSD_BC
LANES = 128
TAIL = 8
NEG = -1e30
LOG2E = math.log2(math.e)
DA_QSCALE = DA_HEAD_DIM ** -0.5 * LOG2E
SW_QSCALE = SW_HEAD_DIM ** -0.5 * LOG2E

VMEM_LIMIT = 56 * 1024 * 1024


def _alibi_slope(i, n):
    return float(2.0 ** (-8.0 * (i + 1) / n))


def _sigmoid(x):
    return 1.0 / (1.0 + jnp.exp(-x))


def _silu(x):
    return x * _sigmoid(x)


def _params(*sem):
    return pltpu.CompilerParams(dimension_semantics=sem, vmem_limit_bytes=VMEM_LIMIT)


def _const_spec(shape):
    nd = len(shape)
    return pl.BlockSpec(shape, lambda *_: (0,) * nd, pipeline_mode=pl.Buffered(1))


DA_TQ = 256
DA_QT = 2
DA_AHEAD = 2
N_SPLIT = 3


def _piece_select(x, lane):
    hi, mid, lo = _split3(x)
    zero = jnp.zeros_like(hi)
    return jnp.where(lane < 2, hi, jnp.where(lane < 4, mid, jnp.where(lane < 2 * N_SPLIT, lo, zero)))


def _diff_attn_kernel(lam_ref, sub_ref, q_ref, k_ref, v_ref, g_ref, o_ref,
                      qa_ref, kaug_ref, dmask_ref, m_ref, l_ref, acc_ref, sbuf_ref, vt_ref, *, lambda_init):
    tq = DA_TQ
    tk = DA_TQ

    lp = lam_ref[...]
    lam = (jnp.exp(jnp.sum(lp[0:1] * lp[1:2], axis=-1, keepdims=True))
           - jnp.exp(jnp.sum(lp[2:3] * lp[3:4], axis=-1, keepdims=True)) + lambda_init)

    lane = lax.broadcasted_iota(jnp.int32, (tq, LANES), 1)
    rowf = lax.broadcasted_iota(jnp.int32, (tq, LANES), 0).astype(F32)
    even = (lane & 1) == 0

    @pl.when(pl.program_id(1) == 0)
    def _():
        krow = lax.broadcasted_iota(jnp.int32, (tk, 2 * tq), 0)
        qcol = lax.broadcasted_iota(jnp.int32, (tk, 2 * tq), 1)
        qcol = jnp.where(qcol >= tq, qcol - tq, qcol)
        dmask_ref[...] = jnp.where(krow <= qcol, 0.0, NEG)
        kext = jnp.where(lane < 2 * N_SPLIT, jnp.where(even, rowf, 1.0), 0.0).astype(BF16)
        qext_t = _piece_select(jnp.where(even, LOG2E, -LOG2E * rowf), lane).astype(F32).T
        for h in range(DA_HEADS):
            qext_h = (qext_t * _alibi_slope(h, DA_HEADS)).astype(BF16)
            qa_ref[h, LANES:2 * LANES, 0:tq] = qext_h
            qa_ref[h, LANES:2 * LANES, tq:2 * tq] = qext_h
        for t in range(k_ref.shape[1] // tk):
            rows = slice(t * tk, (t + 1) * tk)
            for h in range(DA_HEADS):
                kaug_ref[rows, 2 * h * LANES:(2 * h + 1) * LANES] = k_ref[0, rows, h * LANES:(h + 1) * LANES]
                kaug_ref[rows, (2 * h + 1) * LANES:(2 * h + 2) * LANES] = kext
                vt_ref[t, h] = v_ref[0, rows, h * LANES:(h + 1) * LANES].astype(F32).T.astype(BF16)

    def q_tile(qi, rows):
        first_map = lax.broadcasted_iota(jnp.int32, (LANES, tq), 0) < DA_HEAD_DIM
        for h in range(DA_HEADS):
            qt = q_ref[0, rows, h * LANES:(h + 1) * LANES].astype(F32).T
            qa_ref[h, 0:LANES, 0:tq] = jnp.where(first_map, qt, 0.0).astype(BF16)
            qa_ref[h, 0:LANES, tq:2 * tq] = jnp.where(first_map, 0.0, qt).astype(BF16)

        def scores(j, h):
            kst = pl.multiple_of(j * tk, tk)
            ka = kaug_ref[pl.ds(kst, tk), 2 * h * LANES:(2 * h + 2) * LANES]
            return jnp.dot(ka, qa_ref[h], preferred_element_type=F32)

        def tile(j, diagonal):
            kst = pl.multiple_of(j * tk, tk)
            j_next = jnp.maximum(j - 1, 0)
            pending = [sbuf_ref[i] for i in range(DA_AHEAD)]
            for h in range(DA_HEADS):
                hs = slice(h * LANES, (h + 1) * LANES)
                s = pending.pop(0)
                if h + DA_AHEAD < DA_HEADS:
                    pending.append(scores(j, h + DA_AHEAD))
                else:
                    sbuf_ref[h + DA_AHEAD - DA_HEADS] = scores(j_next, h + DA_AHEAD - DA_HEADS)
                vt = vt_ref[j, h]
                if diagonal:
                    s = dmask_ref[...] + s
                tmax = jnp.max(s, axis=0, keepdims=True)
                if diagonal:
                    m_new = tmax
                    p = jnp.exp2(s - m_new)
                    l_ref[h] = jnp.sum(p, axis=0, keepdims=True)
                    pv = jnp.dot(vt, p.astype(BF16), preferred_element_type=F32)
                    acc_ref[h] = pv
                else:
                    m_old = m_ref[h] + _alibi_slope(h, DA_HEADS) * LOG2E * tk
                    m_new = jnp.maximum(m_old, tmax)
                    alpha = jnp.exp2(m_old - m_new)
                    p = jnp.exp2(s - m_new)
                    l_ref[h] = alpha * l_ref[h] + jnp.sum(p, axis=0, keepdims=True)
                    pv = jnp.dot(vt, p.astype(BF16), preferred_element_type=F32)
                    acc_ref[h] = alpha * acc_ref[h] + pv
                m_ref[h] = m_new

        for i in range(DA_AHEAD):
            sbuf_ref[i] = scores(qi, i)

        def diag_body(j, carry):
            tile(j, True)
            return carry

        lax.fori_loop(qi, qi + 1, diag_body, 0)

        def body(i, carry):
            tile(qi - 1 - i, False)
            return carry

        lax.fori_loop(0, qi, body, 0)

        post = sub_ref[...] * (1.0 - lambda_init)
        for h in range(DA_HEADS):
            hs = slice(h * LANES, (h + 1) * LANES)
            o = acc_ref[h] * (1.0 / l_ref[h])
            attn = o[:, 0:tq] - lam * o[:, tq:2 * tq]
            y = attn * lax.rsqrt(jnp.mean(attn * attn, axis=0, keepdims=True) + EPS)
            g = g_ref[0, rows, hs].astype(F32)
            o_ref[0, rows, hs] = (y.T * post * _silu(g)).astype(BF16)

    for sub in range(DA_QT):
        q_tile(DA_QT * pl.program_id(1) + sub, slice(sub * tq, (sub + 1) * tq))


def _diff_attn(proj, lam_params, subln_w, lambda_init):
    b, s, _ = proj.shape
    tq = DA_TQ
    rb = DA_QT * tq
    kern = functools.partial(_diff_attn_kernel, lambda_init=lambda_init)
    return pl.pallas_call(
        kern,
        grid=(b, s // rb),
        in_specs=[
            _const_spec((4, DA_HEAD_DIM)),
            _const_spec((1, 2 * DA_HEAD_DIM)),
            pl.BlockSpec((1, rb, DA_WIDTH), lambda bi, qi: (bi, qi, 0)),
            pl.BlockSpec((1, s, DA_WIDTH), lambda bi, qi: (bi, 0, 1)),
            pl.BlockSpec((1, s, DA_WIDTH), lambda bi, qi: (bi, 0, 2)),
            pl.BlockSpec((1, rb, DA_WIDTH), lambda bi, qi: (bi, qi, 3)),
        ],
        out_specs=pl.BlockSpec((1, rb, DA_WIDTH), lambda bi, qi: (bi, qi, 0)),
        out_shape=jax.ShapeDtypeStruct((b, s, DA_WIDTH), BF16),
        scratch_shapes=[
            pltpu.VMEM((DA_HEADS, 2 * LANES, 2 * tq), BF16),
            pltpu.VMEM((s, 2 * DA_WIDTH), BF16),
            pltpu.VMEM((tq, 2 * tq), F32),
            pltpu.VMEM((DA_HEADS, 1, 2 * tq), F32),
            pltpu.VMEM((DA_HEADS, 1, 2 * tq), F32),
            pltpu.VMEM((DA_HEADS, LANES, 2 * tq), F32),
            pltpu.VMEM((DA_AHEAD, tq, 2 * tq), F32),
            pltpu.VMEM((s // tq, DA_HEADS, LANES, tq), BF16),
        ],
        compiler_params=_params("arbitrary", "arbitrary"),
        name="diff_attn",
    )(lam_params, subln_w, proj, proj, proj, proj)


def _split3(x):
    hi = x.astype(BF16)
    r1 = x - hi.astype(F32)
    mid = r1.astype(BF16)
    lo = (r1 - mid.astype(F32)).astype(BF16)
    return hi, mid, lo


def _expand_heads(x):
    rows = x.shape[0]
    return jnp.concatenate([jnp.broadcast_to(x[:, j:j + 1], (rows, SSD_HEAD_DIM)) for j in range(SSD_HEADS)],
                           axis=1)


def _conv_silu(raw, tail_ref, w_ref, b_ref, keep):
    rows = raw.shape[0]
    tail = tail_ref[...]
    if keep is not None:
        tail = jnp.where(keep, tail, 0.0)
    ext = jnp.concatenate([tail, raw], axis=0)
    tail_ref[...] = raw[rows - TAIL:rows, :]
    acc = b_ref[...] + w_ref[SSD_CONV - 1:SSD_CONV, :] * raw
    for d in range(1, SSD_CONV):
        shifted = pltpu.roll(ext, d, axis=0)[TAIL:TAIL + rows, :]
        acc = acc + w_ref[SSD_CONV - 1 - d:SSD_CONV - d, :] * shifted
    return _silu(acc)


def _ssd_chunk(z_raw, xs_raw, bc_raw, dt_raw, keep, params, tailx_ref, tailb_ref, state_ref):
    cwx_ref, cbx_ref, cwb_ref, cbb_ref, dtb_ref, alog_ref, dskip_ref, nw_ref = params
    L = SSD_CHUNK
    hpg = SSD_HEADS // SSD_GROUPS
    gw = SSD_WIDTH // SSD_GROUPS
    nt = (((1,), (1,)), ((), ()))
    tn = (((0,), (0,)), ((), ()))

    draw = dt_raw + dtb_ref[...]
    dt = jnp.maximum(draw, 0.0) + jnp.log1p(jnp.exp(-jnp.abs(draw)))
    a = -jnp.exp(alog_ref[...])
    da = dt * a
    row = lax.broadcasted_iota(jnp.int32, (L, L), 0)
    col = lax.broadcasted_iota(jnp.int32, (L, L), 1)
    tril = row >= col
    hi, mid, lo = _split3(da)
    ones = tril.astype(BF16)
    yield None
    cs = (jnp.dot(ones, hi, preferred_element_type=F32) + jnp.dot(ones, mid, preferred_element_type=F32)
          + jnp.dot(ones, lo, preferred_element_type=F32))
    dt_e = _expand_heads(dt)

    xs = _conv_silu(xs_raw.astype(F32), tailx_ref, cwx_ref, cbx_ref, keep)
    bc = _conv_silu(bc_raw.astype(F32), tailb_ref, cwb_ref, cbb_ref, keep)
    bcb = bc.astype(BF16)
    cs_t = cs.T
    yield None
    cs_e = _expand_heads(cs)
    bgs = [bcb[:, g * SSD_STATE:(g + 1) * SSD_STATE] for g in range(SSD_GROUPS)]
    cgs = [bcb[:, (SSD_GROUPS + g) * SSD_STATE:(SSD_GROUPS + g + 1) * SSD_STATE] for g in range(SSD_GROUPS)]
    cbs = [lax.dot_general(cgs[g], bgs[g], nt, preferred_element_type=F32) for g in range(SSD_GROUPS)]

    cs_last_e = cs_e[L - 1:L, :]
    xdt = xs * dt_e
    xdt_b = xdt.astype(BF16)
    w_state = (xdt * jnp.exp(cs_last_e - cs_e)).astype(BF16)
    e_in = jnp.exp(cs_e)
    chunk_decay = jnp.exp(cs_last_e)

    lane = lax.broadcasted_iota(jnp.int32, (L, LANES), 1)
    first_head = lane < SSD_HEAD_DIM
    zeros_b = jnp.zeros((L, LANES), BF16)

    y_diag, y_off = [], []
    for g in range(SSD_GROUPS):
        for jp in range(hpg // 2):
            pair = None
            for half in range(2):
                hh = g * hpg + 2 * jp + half
                seg = cs[:, hh:hh + 1] - cs_t[hh:hh + 1, :]
                lm = jnp.exp(jnp.where(tril, seg, NEG))
                mj = (cbs[g] * lm).astype(BF16)
                xp = xdt_b[:, (g * hpg + 2 * jp) * SSD_HEAD_DIM:(g * hpg + 2 * jp + 2) * SSD_HEAD_DIM]
                xp = jnp.where(first_head, xp, zeros_b) if half == 0 else jnp.where(first_head, zeros_b, xp)
                yield None
                t = jnp.dot(mj, xp, preferred_element_type=F32)
                pair = t if pair is None else pair + t
            y_diag.append(pair)
        gs = slice(g * gw, (g + 1) * gw)
        prev = state_ref[g]
        if keep is not None:
            prev = jnp.where(keep, prev, 0.0)
        yield None
        y_off.append(jnp.dot(cgs[g], prev.astype(BF16), preferred_element_type=F32) * e_in[:, gs])
        st = lax.dot_general(bgs[g], w_state[:, gs], tn, preferred_element_type=F32)
        state_ref[g] = prev * chunk_decay[:, gs] + st

    y = jnp.concatenate(y_diag, axis=-1) + jnp.concatenate(y_off, axis=-1) + xs * dskip_ref[...]
    y = y * _silu(z_raw.astype(F32))
    outs = []
    for g in range(SSD_GROUPS):
        yg = y[:, g * gw:(g + 1) * gw]
        outs.append(yg * lax.rsqrt(jnp.mean(yg * yg, axis=-1, keepdims=True) + EPS))
    yield (jnp.concatenate(outs, axis=-1) * nw_ref[...]).astype(BF16)


EVEN_RB = 512
EVEN_TN = 256
SSD_GAPS = 2 + SSD_HEADS + SSD_GROUPS
ATTN_COLS = 4 * DA_WIDTH
SSD_COLS = EVEN_MAIN - ATTN_COLS


def _even_front_kernel(x_ref, nw_ref, w_ref, wdt_ref, cwx_ref, cbx_ref, cwb_ref, cbb_ref, dtb_ref, alog_ref,
                       dskip_ref, snw_ref, o_ref, y_ref,
                       hn_ref, sbuf_ref, dbuf_ref, tailx_ref, tailb_ref, state_ref, *, seq_blocks):
    i = pl.program_id(0)
    L = SSD_CHUNK

    @pl.when(i == 0)
    def _():
        sbuf_ref[...] = jnp.zeros_like(sbuf_ref)
        dbuf_ref[...] = jnp.zeros_like(dbuf_ref)
        tailx_ref[...] = jnp.zeros_like(tailx_ref)
        tailb_ref[...] = jnp.zeros_like(tailb_ref)
        state_ref[...] = jnp.zeros_like(state_ref)

    wr = lax.rem(i, 2)
    rd = 1 - wr
    starts_sequence = lax.rem(i + seq_blocks - 1, seq_blocks) == 0

    x = x_ref[...]
    ms = jnp.mean(x * x, axis=-1, keepdims=True)
    hn_ref[...] = (x * lax.rsqrt(ms + EPS) * nw_ref[...]).astype(BF16)

    def project(j):
        sl = slice(j * EVEN_TN, (j + 1) * EVEN_TN)
        acc = jnp.dot(hn_ref[...], w_ref[:, sl], preferred_element_type=F32)
        if (j + 1) * EVEN_TN <= DA_WIDTH:
            acc = acc * DA_QSCALE
        if (j + 1) * EVEN_TN <= ATTN_COLS:
            o_ref[:, sl] = acc.astype(BF16)
        else:
            sbuf_ref[wr, :, j * EVEN_TN - ATTN_COLS:(j + 1) * EVEN_TN - ATTN_COLS] = acc.astype(BF16)

    n_proj = EVEN_MAIN // EVEN_TN
    n_chunks = EVEN_RB // L
    gaps = n_chunks * SSD_GAPS
    params = (cwx_ref, cbx_ref, cwb_ref, cbb_ref, dtb_ref, alog_ref, dskip_ref, snw_ref)
    done = 0
    gap = 0
    for c in range(n_chunks):
        rows = slice(c * L, (c + 1) * L)
        keep = jnp.logical_not(starts_sequence) if c == 0 else None
        for out in _ssd_chunk(sbuf_ref[rd, rows, 0:SSD_WIDTH], sbuf_ref[rd, rows, SSD_WIDTH:2 * SSD_WIDTH],
                              sbuf_ref[rd, rows, 2 * SSD_WIDTH:SSD_COLS], dbuf_ref[rd, rows, :],
                              keep, params, tailx_ref, tailb_ref, state_ref):
            if out is None:
                gap += 1
                while done < n_proj and done * gaps < gap * n_proj:
                    project(done)
                    done += 1
            else:
                y_ref[rows, :] = out
    while done < n_proj:
        project(done)
        done += 1
    dbuf_ref[wr] = jnp.dot(hn_ref[...], wdt_ref[...], preferred_element_type=F32)


def _even_front(x2, norm_w, w_main, w_dt, cw_x, cb_x, cw_b, cb_b, dt_bias, a_log, d_skip_e, ssd_norm_w, seq_len):
    m = x2.shape[0]
    rb = EVEN_RB
    nb = m // rb
    kern = functools.partial(_even_front_kernel, seq_blocks=seq_len // rb)
    return pl.pallas_call(
        kern,
        grid=(nb + 1,),
        in_specs=[
            pl.BlockSpec((rb, D_MODEL), lambda i: (jnp.minimum(i, nb - 1), 0)),
            _const_spec((1, D_MODEL)),
            _const_spec((D_MODEL, EVEN_MAIN)),
            _const_spec((D_MODEL, LANES)),
            _const_spec((SSD_CONV, SSD_WIDTH)),
            _const_spec((1, SSD_WIDTH)),
            _const_spec((SSD_CONV, SSD_BC)),
            _const_spec((1, SSD_BC)),
            _const_spec((1, LANES)),
            _const_spec((1, LANES)),
            _const_spec((1, SSD_WIDTH)),
            _const_spec((1, SSD_WIDTH)),
        ],
        out_specs=[
            pl.BlockSpec((rb, ATTN_COLS), lambda i: (jnp.minimum(i, nb - 1), 0)),
            pl.BlockSpec((rb, SSD_WIDTH), lambda i: (jnp.maximum(i - 1, 0), 0)),
        ],
        out_shape=[
            jax.ShapeDtypeStruct((m, ATTN_COLS), BF16),
            jax.ShapeDtypeStruct((m, SSD_WIDTH), BF16),
        ],
        scratch_shapes=[
            pltpu.VMEM((rb, D_MODEL), BF16),
            pltpu.VMEM((2, rb, SSD_COLS), BF16),
            pltpu.VMEM((2, rb, LANES), F32),
            pltpu.VMEM((TAIL, SSD_WIDTH), F32),
            pltpu.VMEM((TAIL, SSD_BC), F32),
            pltpu.VMEM((SSD_GROUPS, SSD_STATE, SSD_WIDTH // SSD_GROUPS), F32),
        ],
        compiler_params=_params("arbitrary"),
        name="even_front",
    )(x2, norm_w, w_main, w_dt, cw_x, cb_x, cw_b, cb_b, dt_bias, a_log, d_skip_e, ssd_norm_w)


BACK_RB = 512
BACK_TN = 256
ODD_IN = 2 * SW_WIDTH + 2 * LANES


def _back_kernel(x_ref, ya_ref, yb_ref, wa_ref, wb_ref, nw_ref, win_ref, bin_ref, sink_ref, wo_ref, fw_ref,
                 o_ref, x1_buf, qg_buf, kv_buf, qq_ref, ka_ref, mask_ref, gated_ref, *, seq_blocks):
    W = WINDOW
    nw = BACK_RB // W
    i = pl.program_id(0)
    hpg = SW_GQ

    lane = lax.broadcasted_iota(jnp.int32, (W, LANES), 1)
    even = (lane & 1) == 0

    @pl.when(i == 0)
    def _():
        x1_buf[...] = jnp.zeros_like(x1_buf)
        qg_buf[...] = jnp.zeros_like(qg_buf)
        kv_buf[...] = jnp.zeros_like(kv_buf)
        krel = lax.broadcasted_iota(jnp.int32, (2 * W, W), 0)
        qrel = lax.broadcasted_iota(jnp.int32, (2 * W, W), 1) + W
        dist = qrel - krel
        band = (dist >= 0) & (dist < W)
        mask_ref[0] = jnp.where(band, 0.0, NEG)
        mask_ref[1] = jnp.where(band & (krel >= W), 0.0, NEG)
        lane2 = lax.broadcasted_iota(jnp.int32, (2 * W, LANES), 1)
        krow = lax.broadcasted_iota(jnp.int32, (2 * W, LANES), 0).astype(F32)
        kext = jnp.where(lane2 < 2 * N_SPLIT, jnp.where((lane2 & 1) == 0, krow, 1.0), 0.0).astype(BF16)
        qrow = lax.broadcasted_iota(jnp.int32, (W, LANES), 0).astype(F32) + float(W)
        for g in range(SW_KV_HEADS):
            for w in range(nw):
                ka_ref[w, g, :, LANES:2 * LANES] = kext
            for hl in range(hpg):
                slope2 = _alibi_slope(g * hpg + hl, SW_HEADS) * LOG2E
                qext_t = _piece_select(jnp.where(even, slope2, -slope2 * qrow), lane).astype(F32).T.astype(BF16)
                for w in range(nw):
                    qq_ref[w, g, LANES:2 * LANES, hl * W:(hl + 1) * W] = qext_t

    wr2 = lax.rem(i, 2)
    rd2 = 1 - wr2
    wr3 = lax.rem(i, 3)
    own3 = lax.rem(i + 2, 3)
    prev3 = lax.rem(i + 1, 3)
    first_block = lax.rem(i + seq_blocks - 1, seq_blocks) == 0

    proj_a = jnp.dot(ya_ref[...], wa_ref[...], preferred_element_type=F32)
    proj_b = jnp.dot(yb_ref[...], wb_ref[...], preferred_element_type=F32)

    low2 = lax.broadcasted_iota(jnp.int32, (2 * W, LANES), 1) < SW_HEAD_DIM
    low_t = lax.broadcasted_iota(jnp.int32, (LANES, W), 0) < SW_HEAD_DIM
    scores, vblks, masks = [], [], []
    for w in range(nw):
        rows = slice(w * W, (w + 1) * W)
        kv_prev = kv_buf[prev3, (nw - 1) * W:nw * W, :] if w == 0 else kv_buf[own3, (w - 1) * W:w * W, :]
        kv = jnp.concatenate([kv_prev, kv_buf[own3, rows, :]], axis=0)
        k2 = kv[:, 0:LANES].astype(F32)
        vblks.append(kv[:, LANES:2 * LANES])
        k2r = pltpu.roll(k2, SW_HEAD_DIM, axis=1)
        ka_ref[w, 0, :, 0:LANES] = jnp.where(low2, k2, k2r).astype(BF16)
        ka_ref[w, 1, :, 0:LANES] = jnp.where(low2, k2r, k2).astype(BF16)
        for hp in range(SW_HEADS // 2):
            g, hl = (2 * hp) // hpg, (2 * hp) % hpg
            qt = qg_buf[rd2, rows, hp * LANES:(hp + 1) * LANES].astype(F32).T
            qq_ref[w, g, 0:LANES, hl * W:(hl + 1) * W] = jnp.where(low_t, qt, 0.0).astype(BF16)
            qq_ref[w, g, 0:LANES, (hl + 1) * W:(hl + 2) * W] = jnp.where(low_t, 0.0, qt).astype(BF16)
        masks.append(mask_ref[jnp.where(first_block, 1, 0)] if w == 0 else mask_ref[0])
        scores.append([jnp.dot(ka_ref[w, g], qq_ref[w, g], preferred_element_type=F32)
                       for g in range(SW_KV_HEADS)])

    x1 = x_ref[...] + proj_a + proj_b
    x1_buf[wr2] = x1
    ms = jnp.mean(x1 * x1, axis=-1, keepdims=True)
    hn = (x1 * lax.rsqrt(ms + EPS) * nw_ref[...]).astype(BF16)

    def project(j):
        sl = slice(j * BACK_TN, (j + 1) * BACK_TN)
        acc = jnp.dot(hn, win_ref[:, sl], preferred_element_type=F32) + bin_ref[:, sl]
        if (j + 1) * BACK_TN <= SW_WIDTH:
            acc = acc * SW_QSCALE
        if (j + 1) * BACK_TN <= 2 * SW_WIDTH:
            qg_buf[wr2, :, sl] = acc.astype(BF16)
        else:
            kv_buf[wr3] = acc.astype(BF16)

    n_proj = ODD_IN // BACK_TN
    done = 0

    for w in range(nw):
        rows = slice(w * W, (w + 1) * W)
        for g in range(SW_KV_HEADS):
            probs, inv = [], []
            for hl in range(hpg):
                h = g * hpg + hl
                s = masks[w] + scores[w][g][:, hl * W:(hl + 1) * W]
                sink2 = sink_ref[0:1, h:h + 1] * LOG2E
                m = jnp.maximum(jnp.max(s, axis=0, keepdims=True), sink2)
                p = jnp.exp2(s - m)
                inv.append(1.0 / (jnp.sum(p, axis=0, keepdims=True) + jnp.exp2(sink2 - m)))
                probs.append(p.astype(BF16))
                if done < n_proj and (hl & 1) == 1:
                    project(done)
                    done += 1
            pt = jnp.concatenate(probs, axis=1)
            ot = lax.dot_general(vblks[w], pt, (((0,), (0,)), ((), ())), preferred_element_type=F32)
            ot = ot[g * SW_HEAD_DIM:(g + 1) * SW_HEAD_DIM, :] * jnp.concatenate(inv, axis=1)
            for k in range(hpg // 2):
                pair = jnp.concatenate([ot[:, (2 * k) * W:(2 * k + 1) * W],
                                        ot[:, (2 * k + 1) * W:(2 * k + 2) * W]], axis=0)
                hs = slice((g * hpg // 2 + k) * LANES, (g * hpg // 2 + k + 1) * LANES)
                gate = qg_buf[rd2, rows, SW_WIDTH + hs.start:SW_WIDTH + hs.stop].astype(F32)
                gated_ref[rows, hs] = (pair.T * _silu(gate)).astype(BF16)
    while done < n_proj:
        project(done)
        done += 1

    x2 = x1_buf[rd2] + jnp.dot(gated_ref[...], wo_ref[...], preferred_element_type=F32)
    ms2 = jnp.mean(x2 * x2, axis=-1, keepdims=True)
    o_ref[...] = x2 * lax.rsqrt(ms2 + EPS) * fw_ref[...]


def _back(x2, ya, yb, w_a, w_b, norm_w, w_in, b_in, sinks, w_out, final_w, seq_len):
    m = x2.shape[0]
    W = WINDOW
    rb = BACK_RB
    nb = m // rb
    cur = lambda i: (jnp.minimum(i, nb - 1), 0)
    kern = functools.partial(_back_kernel, seq_blocks=seq_len // rb)
    return pl.pallas_call(
        kern,
        grid=(nb + 1,),
        in_specs=[
            pl.BlockSpec((rb, D_MODEL), cur),
            pl.BlockSpec((rb, DA_WIDTH), cur),
            pl.BlockSpec((rb, SSD_WIDTH), cur),
            _const_spec((DA_WIDTH, D_MODEL)),
            _const_spec((SSD_WIDTH, D_MODEL)),
            _const_spec((1, D_MODEL)),
            _const_spec((D_MODEL, ODD_IN)),
            _const_spec((1, ODD_IN)),
            _const_spec((1, LANES)),
            _const_spec((SW_WIDTH, D_MODEL)),
            _const_spec((1, D_MODEL)),
        ],
        out_specs=pl.BlockSpec((rb, D_MODEL), lambda i: (jnp.maximum(i - 1, 0), 0)),
        out_shape=jax.ShapeDtypeStruct((m, D_MODEL), F32),
        scratch_shapes=[
            pltpu.VMEM((2, rb, D_MODEL), F32),
            pltpu.VMEM((2, rb, 2 * SW_WIDTH), BF16),
            pltpu.VMEM((3, rb, 2 * LANES), BF16),
            pltpu.VMEM((rb // W, SW_KV_HEADS, 2 * LANES, SW_GQ * W), BF16),
            pltpu.VMEM((rb // W, SW_KV_HEADS, 2 * W, 2 * LANES), BF16),
            pltpu.VMEM((2, 2 * W, W), F32),
            pltpu.VMEM((rb, SW_WIDTH), BF16),
        ],
        compiler_params=_params("arbitrary"),
        name="back",
    )(x2, ya, yb, w_a, w_b, norm_w, w_in, b_in, sinks, w_out, final_w)


def _even_layer(x, norm_w, w_in, conv_w, conv_b, dt_bias, a_log, d_skip, ssd_norm_w,
                lq1, lk1, lq2, lk2, subln_w, lambda_init):
    b, s, d = x.shape
    w_main = w_in.astype(BF16)
    w_dt = jnp.pad(w_in[:, EVEN_MAIN:], ((0, 0), (0, LANES - SSD_HEADS))).astype(BF16)
    pad = lambda v: jnp.pad(v.astype(F32), (0, LANES - SSD_HEADS)).reshape(1, LANES)
    proj, y_b = _even_front(x.reshape(b * s, d), norm_w.reshape(1, d), w_main, w_dt,
                            conv_w[:, :SSD_WIDTH], conv_b[:SSD_WIDTH].reshape(1, -1),
                            conv_w[:, SSD_WIDTH:], conv_b[SSD_WIDTH:].reshape(1, -1),
                            pad(dt_bias), pad(a_log),
                            jnp.repeat(d_skip.astype(F32), SSD_HEAD_DIM).reshape(1, -1),
                            ssd_norm_w.reshape(1, -1).astype(F32), s)
    proj = proj.reshape(b, s, ATTN_COLS)

    lam_params = jnp.stack([lq1, lk1, lq2, lk2]).astype(F32)
    y_a = _diff_attn(proj, lam_params, subln_w.reshape(1, -1).astype(F32), lambda_init)
    return y_a, y_b


def kernel(x, norm_a, w_in_a, conv_w_a, conv_b_a, dt_bias_a, a_log_a, d_skip_a, ssd_norm_a,
           lambda_q1_a, lambda_k1_a, lambda_q2_a, lambda_k2_a, subln_a, w_out_a,
           norm_c, w_in_c, b_in_c, sinks_c, w_out_c, final_norm):
    b, s, d = x.shape
    lambda_init = 0.8 - 0.6 * math.exp(-0.3 * 0)
    y_a, y_b = _even_layer(x, norm_a[0], w_in_a[0], conv_w_a[0], conv_b_a[0], dt_bias_a[0], a_log_a[0],
                           d_skip_a[0], ssd_norm_a[0], lambda_q1_a[0], lambda_k1_a[0], lambda_q2_a[0],
                           lambda_k2_a[0], subln_a[0], lambda_init)

    kv0 = SW_WIDTH
    g0 = SW_WIDTH + 2 * SW_KV_HEADS * SW_HEAD_DIM
    perm = lambda w: jnp.concatenate([w[..., :kv0], w[..., g0:], w[..., kv0:g0]], axis=-1)
    w_in = perm(w_in_c[0]).astype(BF16)
    b_in = perm(b_in_c[0]).reshape(1, -1).astype(F32)
    w_out = w_out_a[0].astype(BF16)
    sinks = jnp.pad(sinks_c[0].astype(F32), (0, LANES - SW_HEADS)).reshape(1, LANES)
    out = _back(x.reshape(b * s, d), y_a.reshape(b * s, -1), y_b.reshape(b * s, -1),
                w_out[:DA_WIDTH], w_out[DA_WIDTH:], norm_c[0].reshape(1, d), w_in, b_in, sinks,
                w_out_c[0].astype(BF16), final_norm.reshape(1, d), s)
    return out.reshape(b, s, d)
```
